```python
import jax, jax.numpy as jnp
from jax import lax
import numpy as np


D_MODEL = 1024
BATCH = 8
SEQ = 8192
DEPTH = 1
DEC_BATCH = 8
DEC_SEQ = 2048
PAST_LEN = 128

GRID_W = 64
PLE_DIM = 256
POOL_WIDTH = 1024
POOL_GROUPS = 4
POOL_GROUP_WIDTH = POOL_WIDTH // POOL_GROUPS
POOL_WINDOWS = (2, 4, 8, 16)
N_Q_HEADS = 16
N_KV_HEADS = 4
HEADS_PER_KV = N_Q_HEADS // N_KV_HEADS
HEAD_DIM = 64
Q_WIDTH = N_Q_HEADS * HEAD_DIM
KV_WIDTH = N_KV_HEADS * HEAD_DIM
ROPE_AXIS_DIM = HEAD_DIM // 2
ROPE_BASE = 10000.0
Q_BLOCK = 128
EPS = 1e-6
IN_SPLITS = (POOL_WIDTH, POOL_WIDTH, Q_WIDTH, KV_WIDTH, KV_WIDTH, Q_WIDTH, D_MODEL, D_MODEL)
IN_WIDTH = sum(IN_SPLITS)
SPLIT_POINTS = tuple(int(v) for v in np.cumsum(IN_SPLITS)[:-1])

kernel_name = 'hybrid_pool_gqa_encoder'


def rms_norm(x, g):
    xf = x.astype(jnp.float32)
    y = xf * lax.rsqrt(jnp.mean(xf * xf, axis=-1, keepdims=True) + EPS)
    return (y * g.astype(jnp.float32)).astype(x.dtype)


def rope_1d(x, pos):
    dim = x.shape[-1]
    freqs = ROPE_BASE ** (-jnp.arange(0, dim, 2, dtype=jnp.float32) / dim)
    ang = pos[:, None] * freqs[None, :]
    cos = jnp.cos(ang)[None, :, None, :]
    sin = jnp.sin(ang)[None, :, None, :]
    xf = x.astype(jnp.float32)
    x1, x2 = xf[..., : dim // 2], xf[..., dim // 2:]
    out = jnp.concatenate([x1 * cos - x2 * sin, x2 * cos + x1 * sin], axis=-1)
    return out.astype(x.dtype)


def axial_rope(x, pos_row, pos_col):
    return jnp.concatenate([rope_1d(x[..., :ROPE_AXIS_DIM], pos_row),
                            rope_1d(x[..., ROPE_AXIS_DIM:], pos_col)], axis=-1)


def pool_mixer(u, pool_w, pool_scale):
    B, T, _ = u.shape
    uf = u.astype(jnp.float32)
    c = jnp.concatenate([jnp.zeros((B, 1, POOL_WIDTH), jnp.float32), jnp.cumsum(uf, axis=1)], axis=1)
    t = jnp.arange(T)
    groups = []
    for gi, w in enumerate(POOL_WINDOWS):
        sl = slice(gi * POOL_GROUP_WIDTH, (gi + 1) * POOL_GROUP_WIDTH)
        lo = jnp.clip(t - w // 2, 0, T)
        hi = jnp.clip(t + w - w // 2, 0, T)
        cnt = (hi - lo).astype(jnp.float32)
        cg = c[:, :, sl]
        mean = (cg[:, hi] - cg[:, lo]) / cnt[None, :, None]
        groups.append(mean - uf[..., sl])
    pooled = jnp.stack(groups, axis=2).astype(u.dtype)
    mixed = jnp.einsum('btgc,gcd->btgd', pooled, pool_w).reshape(B, T, POOL_WIDTH)
    return mixed * pool_scale


def block_attention(q, k, v):
    B, T, _, _ = q.shape
    n_blk = T // Q_BLOCK
    scale = HEAD_DIM ** -0.5
    qb = q.reshape(B, n_blk, Q_BLOCK, N_KV_HEADS, HEADS_PER_KV, HEAD_DIM)
    qb = jnp.moveaxis(qb, 1, 0)

    def one_block(qi):
        s = jnp.einsum('bqkgd,bskd->bkgqs', qi, k, preferred_element_type=jnp.float32) * scale
        p = jax.nn.softmax(s, axis=-1)
        return jnp.einsum('bkgqs,bskd->bqkgd', p.astype(v.dtype), v)

    o = lax.map(one_block, qb)
    return jnp.moveaxis(o, 0, 1).reshape(B, T, Q_WIDTH)


def encoder_layer(x, p_i, norm_pre, w_in, pool_w, pool_scale, w_branch_a, q_norm, k_norm,
                  w_branch_b, w_out, norm_post, ple_norm, w_ple_gate, w_ple_in):
    B, T, _ = x.shape
    rows = T // GRID_W
    pos_row = jnp.repeat(jnp.arange(rows, dtype=jnp.float32), GRID_W)
    pos_col = jnp.tile(jnp.arange(GRID_W, dtype=jnp.float32), rows)

    h = rms_norm(x, norm_pre)
    u = h @ w_in
    ua, za, q, k, v, zb, ma, mb = jnp.split(u, SPLIT_POINTS, axis=-1)

    a = pool_mixer(ua, pool_w, pool_scale) * jax.nn.silu(za)
    a = a @ w_branch_a

    q = rms_norm(q.reshape(B, T, N_Q_HEADS, HEAD_DIM), q_norm)
    k = rms_norm(k.reshape(B, T, N_KV_HEADS, HEAD_DIM), k_norm)
    v = v.reshape(B, T, N_KV_HEADS, HEAD_DIM)
    q = axial_rope(q, pos_row, pos_col)
    k = axial_rope(k, pos_row, pos_col)
    b = block_attention(q, k, v) * jax.nn.silu(zb)
    b = b @ w_branch_b

    merged = jax.nn.sigmoid(ma) * a + jax.nn.sigmoid(mb) * b
    x = x + rms_norm(merged @ w_out, norm_post)

    gate = jax.nn.sigmoid(rms_norm(x, ple_norm) @ w_ple_gate)
    return x + gate * (p_i @ w_ple_in)


def trunk(x, p, norm_pre, w_in, pool_w, pool_scale, w_branch_a, q_norm, k_norm,
          w_branch_b, w_out, norm_post, ple_norm, w_ple_gate, w_ple_in):
    for i in range(DEPTH):
        x = encoder_layer(x, p[i], norm_pre[i], w_in[i], pool_w[i], pool_scale[i], w_branch_a[i],
                          q_norm[i], k_norm[i], w_branch_b[i], w_out[i], norm_post[i],
                          ple_norm[i], w_ple_gate[i], w_ple_in[i])
    return x


def setup_inputs(seed: int = 0) -> dict:
    key = jax.random.key(seed)
    ks = jax.random.split(key, 20)
    f32 = jnp.float32

    def nrm(k, shape, scale=1.0):
        return jax.random.normal(k, shape, f32) * scale

    return {
        'x_prompt': nrm(ks[0], (BATCH, SEQ, D_MODEL)),
        'x_sample': nrm(ks[1], (DEC_BATCH, DEC_SEQ, D_MODEL)),
        'p_prompt': nrm(ks[2], (DEPTH, BATCH, SEQ, PLE_DIM)),
        'p_sample': nrm(ks[3], (DEPTH, DEC_BATCH, DEC_SEQ, PLE_DIM)),
        'norm_pre': 1.0 + nrm(ks[4], (DEPTH, D_MODEL), 0.05),
        'w_in': nrm(ks[5], (DEPTH, D_MODEL, IN_WIDTH), D_MODEL ** -0.5),
        'pool_w': nrm(ks[6], (DEPTH, POOL_GROUPS, POOL_GROUP_WIDTH, POOL_GROUP_WIDTH), POOL_GROUP_WIDTH ** -0.5),
        'pool_scale': 1.0 + nrm(ks[7], (DEPTH, POOL_WIDTH), 0.05),
        'w_branch_a': nrm(ks[8], (DEPTH, POOL_WIDTH, D_MODEL), POOL_WIDTH ** -0.5),
        'q_norm': 1.0 + nrm(ks[9], (DEPTH, HEAD_DIM), 0.05),
        'k_norm': 1.0 + nrm(ks[10], (DEPTH, HEAD_DIM), 0.05),
        'w_branch_b': nrm(ks[11], (DEPTH, Q_WIDTH, D_MODEL), Q_WIDTH ** -0.5),
        'w_out': nrm(ks[12], (DEPTH, D_MODEL, D_MODEL), D_MODEL ** -0.5),
        'norm_post': 1.0 + nrm(ks[13], (DEPTH, D_MODEL), 0.05),
        'ple_norm': 1.0 + nrm(ks[14], (DEPTH, D_MODEL), 0.05),
        'w_ple_gate': nrm(ks[15], (DEPTH, D_MODEL, D_MODEL), D_MODEL ** -0.5),
        'w_ple_in': nrm(ks[16], (DEPTH, PLE_DIM, D_MODEL), PLE_DIM ** -0.5),
    }


def reference(x_prompt, x_sample, p_prompt, p_sample, norm_pre, w_in, pool_w, pool_scale, w_branch_a,
              q_norm, k_norm, w_branch_b, w_out, norm_post, ple_norm, w_ple_gate, w_ple_in):
    y_prompt = trunk(x_prompt, p_prompt, norm_pre, w_in, pool_w, pool_scale, w_branch_a, q_norm, k_norm,
                     w_branch_b, w_out, norm_post, ple_norm, w_ple_gate, w_ple_in)
    y_sample = trunk(x_sample, p_sample, norm_pre, w_in, pool_w, pool_scale, w_branch_a, q_norm, k_norm,
                     w_branch_b, w_out, norm_post, ple_norm, w_ple_gate, w_ple_in)
    return (y_prompt, y_sample)
```

```python
import functools
import math

import jax
import jax.numpy as jnp
from jax import lax
from jax.experimental import pallas as pl
from jax.experimental.pallas import tpu as pltpu

D_MODEL = 1024
GRID_W = 64
PLE_DIM = 256
POOL_WIDTH = 1024
POOL_GROUPS = 4
POOL_GROUP_WIDTH = POOL_WIDTH // POOL_GROUPS
POOL_WINDOWS = (2, 4, 8, 16)
N_Q_HEADS = 16
N_KV_HEADS = 4
HEADS_PER_KV = N_Q_HEADS // N_KV_HEADS
HEAD_DIM = 64
Q_WIDTH = N_Q_HEADS * HEAD_DIM
KV_WIDTH = N_KV_HEADS * HEAD_DIM
ROPE_AXIS_DIM = HEAD_DIM // 2
ROPE_HALF = ROPE_AXIS_DIM // 2
ROPE_BASE = 10000.0
EPS = 1e-6
IN_SPLITS = (POOL_WIDTH, POOL_WIDTH, Q_WIDTH, KV_WIDTH, KV_WIDTH, Q_WIDTH, D_MODEL, D_MODEL)
IN_WIDTH = sum(IN_SPLITS)
IN_OFFSETS = tuple(sum(IN_SPLITS[:i]) for i in range(len(IN_SPLITS)))

V7X_LANES = 128
V7X_SUBLANES = 8
V7X_VMEM_LIMIT_BYTES = 56 * 1024 * 1024

SCORE_SCALE_LOG2 = (HEAD_DIM ** -0.5) * math.log2(math.e)

POOL_HALO = 8


def _tiles(seq_len):
    assert seq_len % 256 == 0
    return dict(proj_rows=256, post_rows=256, q_tile=256, kv_tile=256)


def _rope_tables(seq_len, scale):
    t = jnp.arange(seq_len)
    pos_row = (t // GRID_W).astype(jnp.float32)
    pos_col = (t % GRID_W).astype(jnp.float32)
    freqs = ROPE_BASE ** (-jnp.arange(0, ROPE_AXIS_DIM, 2, dtype=jnp.float32) / ROPE_AXIS_DIM)
    ang_row = pos_row[:, None] * freqs[None, :]
    ang_col = pos_col[:, None] * freqs[None, :]
    zeros = jnp.zeros_like(ang_row)

    def head(fn_row, fn_col):
        return jnp.concatenate([fn_row, fn_row, fn_col, fn_col], axis=-1)

    cos = head(jnp.cos(ang_row), jnp.cos(ang_col))
    sin_lo = jnp.concatenate([zeros, jnp.sin(ang_row), zeros, jnp.sin(ang_col)], axis=-1)
    sin_hi = jnp.concatenate([-jnp.sin(ang_row), zeros, -jnp.sin(ang_col), zeros], axis=-1)
    two = lambda a: jnp.concatenate([a, a], axis=-1) * scale
    return two(cos), two(sin_lo), two(sin_hi)


def _head_norm_rope(x, gain, cos, sin_lo, sin_hi):
    lane = lax.broadcasted_iota(jnp.int32, x.shape, 1)
    first = lane < HEAD_DIM
    x2 = x * x
    ss_a = jnp.sum(jnp.where(first, x2, 0.0), axis=-1, keepdims=True)
    ss_b = jnp.sum(jnp.where(first, 0.0, x2), axis=-1, keepdims=True)
    ms = jnp.where(first, ss_a, ss_b) * (1.0 / HEAD_DIM)
    xn = x * lax.rsqrt(ms + EPS) * gain
    return (xn * cos
            + pltpu.roll(xn, ROPE_HALF, 1) * sin_lo
            + pltpu.roll(xn, V7X_LANES - ROPE_HALF, 1) * sin_hi)


def _in_proj_kernel(x_ref, g_ref, w_ref, qg_ref, kg_ref, cq_ref, slq_ref, shq_ref, ck_ref, slk_ref, shk_ref,
                    ua_ref, za_ref, q_ref, k_ref, v_ref, zb_ref, ma_ref, mb_ref):
    x = x_ref[...]
    h = x * lax.rsqrt(jnp.mean(x * x, axis=-1, keepdims=True) + EPS) * g_ref[...]
    h = h.astype(jnp.bfloat16)

    def proj(idx):
        lo = IN_OFFSETS[idx]
        return jnp.dot(h, w_ref[:, lo:lo + IN_SPLITS[idx]], preferred_element_type=jnp.float32)

    ua_ref[...] = proj(0)
    za_ref[...] = proj(1).astype(za_ref.dtype)

    q = proj(2)
    cq, slq, shq = cq_ref[...], slq_ref[...], shq_ref[...]
    for c in range(Q_WIDTH // V7X_LANES):
        sl = slice(c * V7X_LANES, (c + 1) * V7X_LANES)
        q_ref[:, sl] = _head_norm_rope(q[:, sl], qg_ref[...], cq, slq, shq).astype(q_ref.dtype)

    k = proj(3)
    ck, slk, shk = ck_ref[...], slk_ref[...], shk_ref[...]
    for c in range(KV_WIDTH // V7X_LANES):
        sl = slice(c * V7X_LANES, (c + 1) * V7X_LANES)
        k_ref[:, sl] = _head_norm_rope(k[:, sl], kg_ref[...], ck, slk, shk).astype(k_ref.dtype)

    v_ref[...] = proj(4).astype(v_ref.dtype)
    zb_ref[...] = proj(5).astype(zb_ref.dtype)
    ma_ref[...] = proj(6).astype(ma_ref.dtype)
    mb_ref[...] = proj(7).astype(mb_ref.dtype)


def _in_proj(x2d, seq_len, norm_pre, w_in_bf16, q_gain, k_gain, tiles):
    n_rows = x2d.shape[0]
    tm = tiles["proj_rows"]
    seq_tiles = seq_len // tm
    cq, slq, shq = _rope_tables(seq_len, SCORE_SCALE_LOG2)
    ck, slk, shk = _rope_tables(seq_len, 1.0)

    row = lambda width: pl.BlockSpec((tm, width), lambda i: (i, 0))
    const = lambda shape: pl.BlockSpec(shape, lambda i: (0, 0))
    table = pl.BlockSpec((tm, V7X_LANES), lambda i: (i % seq_tiles, 0))
    bf16 = jnp.bfloat16
    out_shapes = (
        jax.ShapeDtypeStruct((n_rows, POOL_WIDTH), jnp.float32),
        jax.ShapeDtypeStruct((n_rows, POOL_WIDTH), bf16),
        jax.ShapeDtypeStruct((n_rows, Q_WIDTH), bf16),
        jax.ShapeDtypeStruct((n_rows, KV_WIDTH), bf16),
        jax.ShapeDtypeStruct((n_rows, KV_WIDTH), bf16),
        jax.ShapeDtypeStruct((n_rows, Q_WIDTH), bf16),
        jax.ShapeDtypeStruct((n_rows, D_MODEL), bf16),
        jax.ShapeDtypeStruct((n_rows, D_MODEL), bf16),
    )
    return pl.pallas_call(
        _in_proj_kernel,
        grid=(n_rows // tm,),
        in_specs=[row(D_MODEL), const((1, D_MODEL)), const((D_MODEL, IN_WIDTH)),
                  const((1, V7X_LANES)), const((1, V7X_LANES)),
                  table, table, table, table, table, table],
        out_specs=tuple(row(s.shape[1]) for s in out_shapes),
        out_shape=out_shapes,
        compiler_params=pltpu.CompilerParams(
            dimension_semantics=("parallel",), vmem_limit_bytes=V7X_VMEM_LIMIT_BYTES),
        name="in_proj",
    )(x2d, norm_pre, w_in_bf16, q_gain, k_gain, cq, slq, shq, ck, slk, shk)


def _attention_kernel(q_ref, k_ref, vt_ref, zb_ref, o_ref, m_ref, l_ref, acc_ref, *, n_chunks, kv_tile):
    m_ref[...] = jnp.full(m_ref.shape, -jnp.inf, jnp.float32)
    l_ref[...] = jnp.zeros(l_ref.shape, jnp.float32)
    acc_ref[...] = jnp.zeros(acc_ref.shape, jnp.float32)

    def chunk(c, carry):
        start = pl.multiple_of(c * kv_tile, kv_tile)
        k_c = k_ref[0, 0, pl.ds(start, kv_tile), :]
        vt_c = vt_ref[0, 0, c]
        for hq in range(HEADS_PER_KV):
            s_t = lax.dot_general(k_c, q_ref[0, hq], (((1,), (1,)), ((), ())),
                                  preferred_element_type=jnp.float32)
            m_old = m_ref[hq]
            m_new = jnp.maximum(m_old, jnp.max(s_t, axis=0, keepdims=True))
            alpha = jnp.exp2(m_old - m_new)
            p_t = jnp.exp2(s_t - m_new)
            l_ref[hq] = alpha * l_ref[hq] + jnp.sum(p_t, axis=0, keepdims=True)
            m_ref[hq] = m_new
            pv = jnp.dot(vt_c, p_t.astype(jnp.bfloat16), preferred_element_type=jnp.float32)
            rows = slice(hq * HEAD_DIM, (hq + 1) * HEAD_DIM)
            acc_ref[rows, :] = alpha * acc_ref[rows, :] + pv
        return carry

    lax.fori_loop(0, n_chunks, chunk, 0)

    parts = []
    for hq in range(HEADS_PER_KV):
        rows = slice(hq * HEAD_DIM, (hq + 1) * HEAD_DIM)
        parts.append(acc_ref[rows, :] / l_ref[hq])
    o_t = jnp.concatenate(parts, axis=0)
    zb = zb_ref[...].astype(jnp.float32)
    o_ref[...] = (o_t.T * (zb * jax.nn.sigmoid(zb))).astype(o_ref.dtype)


def _attention(q, k, v, zb, batch, seq_len, tiles):
    tq, tk = tiles["q_tile"], tiles["kv_tile"]
    n_chunks = seq_len // tk
    q_tiles = seq_len // tq
    q_h = q.reshape(batch, seq_len, N_Q_HEADS, HEAD_DIM).transpose(0, 2, 1, 3)
    k_h = k.reshape(batch, seq_len, N_KV_HEADS, HEAD_DIM).transpose(0, 2, 1, 3)
    v_t = v.reshape(batch, n_chunks, tk, N_KV_HEADS, HEAD_DIM).transpose(0, 3, 1, 4, 2)

    kernel = functools.partial(_attention_kernel, n_chunks=n_chunks, kv_tile=tk)
    group_width = HEADS_PER_KV * HEAD_DIM
    return pl.pallas_call(
        kernel,
        grid=(batch, N_KV_HEADS, q_tiles),
        in_specs=[
            pl.BlockSpec((1, HEADS_PER_KV, tq, HEAD_DIM), lambda b, g, i: (b, g, i, 0)),
            pl.BlockSpec((1, 1, seq_len, HEAD_DIM), lambda b, g, i: (b, g, 0, 0)),
            pl.BlockSpec((1, 1, n_chunks, HEAD_DIM, tk), lambda b, g, i: (b, g, 0, 0, 0)),
            pl.BlockSpec((tq, group_width), lambda b, g, i: (b * q_tiles + i, g)),
        ],
        out_specs=pl.BlockSpec((tq, group_width), lambda b, g, i: (b * q_tiles + i, g)),
        out_shape=jax.ShapeDtypeStruct((batch * seq_len, Q_WIDTH), jnp.bfloat16),
        scratch_shapes=[
            pltpu.VMEM((HEADS_PER_KV, 1, tq), jnp.float32),
            pltpu.VMEM((HEADS_PER_KV, 1, tq), jnp.float32),
            pltpu.VMEM((group_width, tq), jnp.float32),
        ],
        compiler_params=pltpu.CompilerParams(
            dimension_semantics=("parallel", "parallel", "parallel"), vmem_limit_bytes=V7X_VMEM_LIMIT_BYTES),
        name="attention",
    )(q_h, k_h, v_t, zb)


def _window_sums(ext, window):
    n = ext.shape[0]
    back = lambda a, s: pltpu.roll(a, s, 0)
    fwd = lambda a, s: pltpu.roll(a, n - s, 0)
    half = window // 2
    run, span = ext, 1
    while span < half:
        run = run + back(run, span)
        span *= 2
    ahead = fwd(run, half - 1) if half > 1 else run
    total = ahead + back(run, 1)
    return total[POOL_HALO:n - POOL_HALO]


def _post_kernel(ua_ref, prev_ref, next_ref, za_ref, b_ref, ma_ref, mb_ref, x_ref, p_ref,
                 pool_w_ref, pool_scale_ref, wa_ref, wb_ref, wo_ref, npost_ref, nple_ref, wg_ref, wp_ref,
                 y_ref, *, seq_len, rows):
    i = pl.program_id(0)
    tiles_per_seq = seq_len // rows
    t0 = (i % tiles_per_seq) * rows
    ua = ua_ref[...]
    prev = jnp.where(t0 == 0, 0.0, prev_ref[...])
    nxt = jnp.where(t0 + rows == seq_len, 0.0, next_ref[...])
    ext = jnp.concatenate([prev, ua, nxt], axis=0)
    t = t0 + lax.broadcasted_iota(jnp.int32, (rows, 1), 0)
    za = za_ref[...].astype(jnp.float32)
    gate_a = za * jax.nn.sigmoid(za)

    a_parts = []
    for gi, w in enumerate(POOL_WINDOWS):
        cols = slice(gi * POOL_GROUP_WIDTH, (gi + 1) * POOL_GROUP_WIDTH)
        total = _window_sums(ext[:, cols], w)
        lo = jnp.maximum(t - w // 2, 0)
        hi = jnp.minimum(t + (w - w // 2), seq_len)
        cnt = (hi - lo).astype(jnp.float32)
        pooled = (total / cnt - ua[:, cols]).astype(jnp.bfloat16)
        mixed = jnp.dot(pooled, pool_w_ref[gi], preferred_element_type=jnp.float32)
        a_parts.append((mixed * pool_scale_ref[:, cols] * gate_a[:, cols]).astype(jnp.bfloat16))
    a_in = jnp.concatenate(a_parts, axis=1)
    a = jnp.dot(a_in, wa_ref[...], preferred_element_type=jnp.float32)
    b = jnp.dot(b_ref[...], wb_ref[...], preferred_element_type=jnp.float32)

    ma = ma_ref[...].astype(jnp.float32)
    mb = mb_ref[...].astype(jnp.float32)
    merged = (jax.nn.sigmoid(ma) * a + jax.nn.sigmoid(mb) * b).astype(jnp.bfloat16)
    o = jnp.dot(merged, wo_ref[...], preferred_element_type=jnp.float32)
    o = o * lax.rsqrt(jnp.mean(o * o, axis=-1, keepdims=True) + EPS) * npost_ref[...]
    x1 = x_ref[...] + o

    xn = x1 * lax.rsqrt(jnp.mean(x1 * x1, axis=-1, keepdims=True) + EPS) * nple_ref[...]
    gate = jax.nn.sigmoid(jnp.dot(xn.astype(jnp.bfloat16), wg_ref[...], preferred_element_type=jnp.float32))
    pe = jnp.dot(p_ref[...].astype(jnp.bfloat16), wp_ref[...], preferred_element_type=jnp.float32)
    y_ref[...] = x1 + gate * pe


def _post(ua, za, b_in, ma, mb, x2d, p2d, seq_len, weights, tiles):
    n_rows = x2d.shape[0]
    tm = tiles["post_rows"]
    halo_blocks = tm // POOL_HALO
    last_halo = n_rows // POOL_HALO - 1
    row = lambda width: pl.BlockSpec((tm, width), lambda i: (i, 0))
    const2 = lambda shape: pl.BlockSpec(shape, lambda i: (0, 0))
    kernel = functools.partial(_post_kernel, seq_len=seq_len, rows=tm)
    return pl.pallas_call(
        kernel,
        grid=(n_rows // tm,),
        in_specs=[
            row(POOL_WIDTH),
            pl.BlockSpec((POOL_HALO, POOL_WIDTH), lambda i: (jnp.maximum(i * halo_blocks - 1, 0), 0)),
            pl.BlockSpec((POOL_HALO, POOL_WIDTH), lambda i: (jnp.minimum((i + 1) * halo_blocks, last_halo), 0)),
            row(POOL_WIDTH), row(Q_WIDTH), row(D_MODEL), row(D_MODEL), row(D_MODEL), row(PLE_DIM),
            pl.BlockSpec((POOL_GROUPS, POOL_GROUP_WIDTH, POOL_GROUP_WIDTH), lambda i: (0, 0, 0)),
            const2((1, POOL_WIDTH)),
            const2((POOL_WIDTH, D_MODEL)), const2((Q_WIDTH, D_MODEL)), const2((D_MODEL, D_MODEL)),
            const2((1, D_MODEL)), const2((1, D_MODEL)),
            const2((D_MODEL, D_MODEL)), const2((PLE_DIM, D_MODEL)),
        ],
        out_specs=row(D_MODEL),
        out_shape=jax.ShapeDtypeStruct((n_rows, D_MODEL), jnp.float32),
        compiler_params=pltpu.CompilerParams(
            dimension_semantics=("parallel",), vmem_limit_bytes=V7X_VMEM_LIMIT_BYTES),
        name="post",
    )(ua, ua, ua, za, b_in, ma, mb, x2d, p2d, *weights)


def _layer(x, p, norm_pre, w_in, pool_w, pool_scale, w_branch_a, q_gain, k_gain, w_branch_b, w_out,
           norm_post, ple_norm, w_ple_gate, w_ple_in):
    batch, seq_len, _ = x.shape
    tiles = _tiles(seq_len)
    x2d = x.reshape(batch * seq_len, D_MODEL)
    p2d = p.reshape(batch * seq_len, PLE_DIM)
    ua, za, q, k, v, zb, ma, mb = _in_proj(x2d, seq_len, norm_pre, w_in, q_gain, k_gain, tiles)
    b_in = _attention(q, k, v, zb, batch, seq_len, tiles)
    weights = (pool_w, pool_scale, w_branch_a, w_branch_b, w_out, norm_post, ple_norm, w_ple_gate, w_ple_in)
    y = _post(ua, za, b_in, ma, mb, x2d, p2d, seq_len, weights, tiles)
    return y.reshape(batch, seq_len, D_MODEL)


def kernel(x_prompt, x_sample, p_prompt, p_sample, norm_pre, w_in, pool_w, pool_scale, w_branch_a,
           q_norm, k_norm, w_branch_b, w_out, norm_post, ple_norm, w_ple_gate, w_ple_in):
    depth = w_in.shape[0]
    bf16 = jnp.bfloat16
    for layer in range(depth):
        params = (
            norm_pre[layer][None, :],
            w_in[layer].astype(bf16),
            pool_w[layer].astype(bf16),
            pool_scale[layer][None, :],
            w_branch_a[layer].astype(bf16),
            jnp.tile(q_norm[layer], V7X_LANES // HEAD_DIM)[None, :],
            jnp.tile(k_norm[layer], V7X_LANES // HEAD_DIM)[None, :],
            w_branch_b[layer].astype(bf16),
            w_out[layer].astype(bf16),
            norm_post[layer][None, :],
            ple_norm[layer][None, :],
            w_ple_gate[layer].astype(bf16),
            w_ple_in[layer].astype(bf16),
        )
        x_prompt = _layer(x_prompt, p_prompt[layer], *params)
        x_sample = _layer(x_sample, p_sample[layer], *params)
    return (x_prompt, x_sample)
```

```python
import functools
import math

import jax
import jax.numpy as jnp
from jax import lax
from jax.experimental import pallas as pl
from jax.experimental.pallas import tpu as pltpu

D_MODEL = 1024
GRID_W = 64
PLE_DIM = 256
POOL_WIDTH = 1024
POOL_GROUPS = 4
POOL_GROUP_WIDTH = POOL_WIDTH // POOL_GROUPS
POOL_WINDOWS = (2, 4, 8, 16)
N_Q_HEADS = 16
N_KV_HEADS = 4
HEADS_PER_KV = N_Q_HEADS // N_KV_HEADS
HEAD_DIM = 64
Q_WIDTH = N_Q_HEADS * HEAD_DIM
KV_WIDTH = N_KV_HEADS * HEAD_DIM
ROPE_AXIS_DIM = HEAD_DIM // 2
ROPE_HALF = ROPE_AXIS_DIM // 2
ROPE_BASE = 10000.0
EPS = 1e-6
IN_SPLITS = (POOL_WIDTH, POOL_WIDTH, Q_WIDTH, KV_WIDTH, KV_WIDTH, Q_WIDTH, D_MODEL, D_MODEL)
IN_WIDTH = sum(IN_SPLITS)
IN_OFFSETS = tuple(sum(IN_SPLITS[:i]) for i in range(len(IN_SPLITS)))

V7X_LANES = 128
V7X_SUBLANES = 8
V7X_VMEM_LIMIT_BYTES = 56 * 1024 * 1024

SCORE_SCALE_LOG2 = (HEAD_DIM ** -0.5) * math.log2(math.e)

POOL_HALO = 8


def _tiles(seq_len):
    assert seq_len % 256 == 0
    return dict(proj_rows=256, post_rows=256, q_tile=256, kv_tile=256)


def _rope_tables(seq_len, scale):
    t = jnp.arange(seq_len)
    pos_row = (t // GRID_W).astype(jnp.float32)
    pos_col = (t % GRID_W).astype(jnp.float32)
    freqs = ROPE_BASE ** (-jnp.arange(0, ROPE_AXIS_DIM, 2, dtype=jnp.float32) / ROPE_AXIS_DIM)
    ang_row = pos_row[:, None] * freqs[None, :]
    ang_col = pos_col[:, None] * freqs[None, :]
    zeros = jnp.zeros_like(ang_row)

    def head(fn_row, fn_col):
        return jnp.concatenate([fn_row, fn_row, fn_col, fn_col], axis=-1)

    cos = head(jnp.cos(ang_row), jnp.cos(ang_col))
    sin_lo = jnp.concatenate([zeros, jnp.sin(ang_row), zeros, jnp.sin(ang_col)], axis=-1)
    sin_hi = jnp.concatenate([-jnp.sin(ang_row), zeros, -jnp.sin(ang_col), zeros], axis=-1)
    two = lambda a: jnp.concatenate([a, a], axis=-1) * scale
    return two(cos), two(sin_lo), two(sin_hi)


def _head_norm_rope(x, gain, cos, sin_lo, sin_hi):
    lane = lax.broadcasted_iota(jnp.int32, x.shape, 1)
    first = lane < HEAD_DIM
    x2 = x * x
    ss_a = jnp.sum(jnp.where(first, x2, 0.0), axis=-1, keepdims=True)
    ss_b = jnp.sum(jnp.where(first, 0.0, x2), axis=-1, keepdims=True)
    ms = jnp.where(first, ss_a, ss_b) * (1.0 / HEAD_DIM)
    xn = x * lax.rsqrt(ms + EPS) * gain
    return (xn * cos
            + pltpu.roll(xn, ROPE_HALF, 1) * sin_lo
            + pltpu.roll(xn, V7X_LANES - ROPE_HALF, 1) * sin_hi)


def _in_proj_kernel(x_ref, g_ref, w_ref, qg_ref, kg_ref, cq_ref, slq_ref, shq_ref, ck_ref, slk_ref, shk_ref,
                    ua_ref, za_ref, q_ref, k_ref, v_ref, zb_ref, ma_ref, mb_ref):
    x = x_ref[...]
    h = x * lax.rsqrt(jnp.mean(x * x, axis=-1, keepdims=True) + EPS) * g_ref[...]
    h = h.astype(jnp.bfloat16)

    def proj(idx):
        lo = IN_OFFSETS[idx]
        return jnp.dot(h, w_ref[:, lo:lo + IN_SPLITS[idx]], preferred_element_type=jnp.float32)

    ua_ref[...] = proj(0)
    za_ref[...] = proj(1).astype(za_ref.dtype)

    q = proj(2)
    cq, slq, shq = cq_ref[...], slq_ref[...], shq_ref[...]
    for c in range(Q_WIDTH // V7X_LANES):
        sl = slice(c * V7X_LANES, (c + 1) * V7X_LANES)
        q_ref[:, sl] = _head_norm_rope(q[:, sl], qg_ref[...], cq, slq, shq).astype(q_ref.dtype)

    k = proj(3)
    ck, slk, shk = ck_ref[...], slk_ref[...], shk_ref[...]
    for c in range(KV_WIDTH // V7X_LANES):
        sl = slice(c * V7X_LANES, (c + 1) * V7X_LANES)
        k_ref[:, sl] = _head_norm_rope(k[:, sl], kg_ref[...], ck, slk, shk).astype(k_ref.dtype)

    v_ref[...] = proj(4).astype(v_ref.dtype)
    zb_ref[...] = proj(5).astype(zb_ref.dtype)
    ma_ref[...] = proj(6).astype(ma_ref.dtype)
    mb_ref[...] = proj(7).astype(mb_ref.dtype)


def _in_proj(x2d, seq_len, norm_pre, w_in_bf16, q_gain, k_gain, tiles):
    n_rows = x2d.shape[0]
    tm = tiles["proj_rows"]
    seq_tiles = seq_len // tm
    cq, slq, shq = _rope_tables(seq_len, SCORE_SCALE_LOG2)
    ck, slk, shk = _rope_tables(seq_len, 1.0)

    row = lambda width: pl.BlockSpec((tm, width), lambda i: (i, 0))
    const = lambda shape: pl.BlockSpec(shape, lambda i: (0, 0))
    table = pl.BlockSpec((tm, V7X_LANES), lambda i: (i % seq_tiles, 0))
    bf16 = jnp.bfloat16
    out_shapes = (
        jax.ShapeDtypeStruct((n_rows, POOL_WIDTH), jnp.float32),
        jax.ShapeDtypeStruct((n_rows, POOL_WIDTH), bf16),
        jax.ShapeDtypeStruct((n_rows, Q_WIDTH), bf16),
        jax.ShapeDtypeStruct((n_rows, KV_WIDTH), bf16),
        jax.ShapeDtypeStruct((n_rows, KV_WIDTH), bf16),
        jax.ShapeDtypeStruct((n_rows, Q_WIDTH), bf16),
        jax.ShapeDtypeStruct((n_rows, D_MODEL), bf16),
        jax.ShapeDtypeStruct((n_rows, D_MODEL), bf16),
    )
    return pl.pallas_call(
        _in_proj_kernel,
        grid=(n_rows // tm,),
        in_specs=[row(D_MODEL), const((1, D_MODEL)), const((D_MODEL, IN_WIDTH)),
                  const((1, V7X_LANES)), const((1, V7X_LANES)),
                  table, table, table, table, table, table],
        out_specs=tuple(row(s.shape[1]) for s in out_shapes),
        out_shape=out_shapes,
        compiler_params=pltpu.CompilerParams(
            dimension_semantics=("parallel",), vmem_limit_bytes=V7X_VMEM_LIMIT_BYTES),
        name="in_proj",
    )(x2d, norm_pre, w_in_bf16, q_gain, k_gain, cq, slq, shq, ck, slk, shk)


ACC_ROWS = HEAD_DIM + 16
CHUNKS_PER_ITER = 8


def _attention_kernel(q_ref, k_ref, vt_ref, zb_ref, o_ref, s0_ref, s1_ref, p0_ref, p1_ref, *acc_refs,
                      n_chunks, kv_tile):
    tq = q_ref.shape[2]
    heads = range(HEADS_PER_KV)
    ones_rows = (lax.broadcasted_iota(jnp.int32, (ACC_ROWS - HEAD_DIM, kv_tile), 0) == 0).astype(jnp.bfloat16)

    def stage1(c, s_ref):
        start = pl.multiple_of(c * kv_tile, kv_tile)
        k_c = k_ref[0, 0, pl.ds(start, kv_tile), :]
        tile_max = []
        for hq in heads:
            s_t = lax.dot_general(k_c, q_ref[0, hq], (((1,), (1,)), ((), ())),
                                  preferred_element_type=jnp.float32)
            s_ref[hq] = s_t
            tile_max.append(jnp.max(s_t, axis=0, keepdims=True))
        return tuple(tile_max)

    def stage2(s_ref, tile_max, m_run, p_ref):
        m_out, alpha = [], []
        for hq in heads:
            m_new = jnp.maximum(m_run[hq], tile_max[hq])
            alpha.append(jnp.exp2(m_run[hq] - m_new))
            p_ref[hq] = jnp.exp2(s_ref[hq] - m_new).astype(jnp.bfloat16)
            m_out.append(m_new)
        return tuple(m_out), tuple(alpha)

    def stage3(c, p_ref, alpha):
        lhs = jnp.concatenate([vt_ref[0, 0, c], ones_rows], axis=0)
        for hq in heads:
            pv = jnp.dot(lhs, p_ref[hq], preferred_element_type=jnp.float32)
            acc_refs[hq][...] = alpha[hq] * acc_refs[hq][...] + pv

    for acc_ref in acc_refs:
        acc_ref[...] = jnp.zeros(acc_ref.shape, jnp.float32)
    p1_ref[...] = jnp.zeros(p1_ref.shape, p1_ref.dtype)
    last = n_chunks - 1

    s_refs, p_refs = (s0_ref, s1_ref), (p0_ref, p1_ref)

    def body(j, carry):
        m_run, tile_max, alpha_prev = carry
        base = CHUNKS_PER_ITER * j
        for u in range(CHUNKS_PER_ITER):
            cur, nxt = u % 2, (u + 1) % 2
            c = base + u
            stage3(jnp.maximum(c - 1, 0), p_refs[nxt], alpha_prev)
            m_run, alpha_prev = stage2(s_refs[cur], tile_max, m_run, p_refs[cur])
            tile_max = stage1(jnp.minimum(c + 1, last), s_refs[nxt])
        return m_run, tile_max, alpha_prev

    init = (tuple(jnp.full((1, tq), -jnp.inf, jnp.float32) for _ in heads),
            stage1(0, s0_ref),
            tuple(jnp.ones((1, tq), jnp.float32) for _ in heads))
    _, _, alpha_last = lax.fori_loop(0, n_chunks // CHUNKS_PER_ITER, body, init)
    stage3(last, p1_ref, alpha_last)

    parts = []
    for hq in heads:
        acc = acc_refs[hq][...]
        parts.append(acc[:HEAD_DIM] / acc[HEAD_DIM:HEAD_DIM + 1])
    o_t = jnp.concatenate(parts, axis=0)
    zb = zb_ref[...].astype(jnp.float32)
    o_ref[...] = (o_t.T * (zb * jax.nn.sigmoid(zb))).astype(o_ref.dtype)


def _attention(q, k, v, zb, batch, seq_len, tiles):
    tq, tk = tiles["q_tile"], tiles["kv_tile"]
    n_chunks = seq_len // tk
    assert n_chunks % CHUNKS_PER_ITER == 0 and CHUNKS_PER_ITER % 2 == 0
    q_tiles = seq_len // tq
    q_h = q.reshape(batch, seq_len, N_Q_HEADS, HEAD_DIM).transpose(0, 2, 1, 3)
    k_h = k.reshape(batch, seq_len, N_KV_HEADS, HEAD_DIM).transpose(0, 2, 1, 3)
    v_t = v.reshape(batch, n_chunks, tk, N_KV_HEADS, HEAD_DIM).transpose(0, 3, 1, 4, 2)

    kernel = functools.partial(_attention_kernel, n_chunks=n_chunks, kv_tile=tk)
    group_width = HEADS_PER_KV * HEAD_DIM
    score_buf = lambda dtype: pltpu.VMEM((HEADS_PER_KV, tk, tq), dtype)
    return pl.pallas_call(
        kernel,
        grid=(batch, N_KV_HEADS, q_tiles),
        in_specs=[
            pl.BlockSpec((1, HEADS_PER_KV, tq, HEAD_DIM), lambda b, g, i: (b, g, i, 0)),
            pl.BlockSpec((1, 1, seq_len, HEAD_DIM), lambda b, g, i: (b, g, 0, 0)),
            pl.BlockSpec((1, 1, n_chunks, HEAD_DIM, tk), lambda b, g, i: (b, g, 0, 0, 0)),
            pl.BlockSpec((tq, group_width), lambda b, g, i: (b * q_tiles + i, g)),
        ],
        out_specs=pl.BlockSpec((tq, group_width), lambda b, g, i: (b * q_tiles + i, g)),
        out_shape=jax.ShapeDtypeStruct((batch * seq_len, Q_WIDTH), jnp.bfloat16),
        scratch_shapes=[score_buf(jnp.float32), score_buf(jnp.float32),
                        score_buf(jnp.bfloat16), score_buf(jnp.bfloat16)]
                       + [pltpu.VMEM((ACC_ROWS, tq), jnp.float32) for _ in range(HEADS_PER_KV)],
        compiler_params=pltpu.CompilerParams(
            dimension_semantics=("parallel", "parallel", "parallel"), vmem_limit_bytes=V7X_VMEM_LIMIT_BYTES),
        name="attention",
    )(q_h, k_h, v_t, zb)


def _window_sums(ext, window):
    n = ext.shape[0]
    back = lambda a, s: pltpu.roll(a, s, 0)
    fwd = lambda a, s: pltpu.roll(a, n - s, 0)
    half = window // 2
    run, span = ext, 1
    while span < half:
        run = run + back(run, span)
        span *= 2
    ahead = fwd(run, half - 1) if half > 1 else run
    total = ahead + back(run, 1)
    return total[POOL_HALO:n - POOL_HALO]


def _post_kernel(ua_ref, prev_ref, next_ref, za_ref, b_ref, ma_ref, mb_ref, x_ref, p_ref,
                 pool_w_ref, pool_scale_ref, wa_ref, wb_ref, wo_ref, npost_ref, nple_ref, wg_ref, wp_ref,
                 y_ref, *, seq_len, rows):
    i = pl.program_id(0)
    tiles_per_seq = seq_len // rows
    t0 = (i % tiles_per_seq) * rows
    ua = ua_ref[...]
    prev = jnp.where(t0 == 0, 0.0, prev_ref[...])
    nxt = jnp.where(t0 + rows == seq_len, 0.0, next_ref[...])
    ext = jnp.concatenate([prev, ua, nxt], axis=0)
    t = t0 + lax.broadcasted_iota(jnp.int32, (rows, 1), 0)
    za = za_ref[...].astype(jnp.float32)
    gate_a = za * jax.nn.sigmoid(za)

    a_parts = []
    for gi, w in enumerate(POOL_WINDOWS):
        cols = slice(gi * POOL_GROUP_WIDTH, (gi + 1) * POOL_GROUP_WIDTH)
        total = _window_sums(ext[:, cols], w)
        lo = jnp.maximum(t - w // 2, 0)
        hi = jnp.minimum(t + (w - w // 2), seq_len)
        cnt = (hi - lo).astype(jnp.float32)
        pooled = (total / cnt - ua[:, cols]).astype(jnp.bfloat16)
        mixed = jnp.dot(pooled, pool_w_ref[gi], preferred_element_type=jnp.float32)
        a_parts.append((mixed * pool_scale_ref[:, cols] * gate_a[:, cols]).astype(jnp.bfloat16))
    a_in = jnp.concatenate(a_parts, axis=1)
    a = jnp.dot(a_in, wa_ref[...], preferred_element_type=jnp.float32)
    b = jnp.dot(b_ref[...], wb_ref[...], preferred_element_type=jnp.float32)

    ma = ma_ref[...].astype(jnp.float32)
    mb = mb_ref[...].astype(jnp.float32)
    merged = (jax.nn.sigmoid(ma) * a + jax.nn.sigmoid(mb) * b).astype(jnp.bfloat16)
    o = jnp.dot(merged, wo_ref[...], preferred_element_type=jnp.float32)
    o = o * lax.rsqrt(jnp.mean(o * o, axis=-1, keepdims=True) + EPS) * npost_ref[...]
    x1 = x_ref[...] + o

    xn = x1 * lax.rsqrt(jnp.mean(x1 * x1, axis=-1, keepdims=True) + EPS) * nple_ref[...]
    gate = jax.nn.sigmoid(jnp.dot(xn.astype(jnp.bfloat16), wg_ref[...], preferred_element_type=jnp.float32))
    pe = jnp.dot(p_ref[...].astype(jnp.bfloat16), wp_ref[...], preferred_element_type=jnp.float32)
    y_ref[...] = x1 + gate * pe


def _post(ua, za, b_in, ma, mb, x2d, p2d, seq_len, weights, tiles):
    n_rows = x2d.shape[0]
    tm = tiles["post_rows"]
    halo_blocks = tm // POOL_HALO
    last_halo = n_rows // POOL_HALO - 1
    row = lambda width: pl.BlockSpec((tm, width), lambda i: (i, 0))
    const2 = lambda shape: pl.BlockSpec(shape, lambda i: (0, 0))
    kernel = functools.partial(_post_kernel, seq_len=seq_len, rows=tm)
    return pl.pallas_call(
        kernel,
        grid=(n_rows // tm,),
        in_specs=[
            row(POOL_WIDTH),
            pl.BlockSpec((POOL_HALO, POOL_WIDTH), lambda i: (jnp.maximum(i * halo_blocks - 1, 0), 0)),
            pl.BlockSpec((POOL_HALO, POOL_WIDTH), lambda i: (jnp.minimum((i + 1) * halo_blocks, last_halo), 0)),
            row(POOL_WIDTH), row(Q_WIDTH), row(D_MODEL), row(D_MODEL), row(D_MODEL), row(PLE_DIM),
            pl.BlockSpec((POOL_GROUPS, POOL_GROUP_WIDTH, POOL_GROUP_WIDTH), lambda i: (0, 0, 0)),
            const2((1, POOL_WIDTH)),
            const2((POOL_WIDTH, D_MODEL)), const2((Q_WIDTH, D_MODEL)), const2((D_MODEL, D_MODEL)),
            const2((1, D_MODEL)), const2((1, D_MODEL)),
            const2((D_MODEL, D_MODEL)), const2((PLE_DIM, D_MODEL)),
        ],
        out_specs=row(D_MODEL),
        out_shape=jax.ShapeDtypeStruct((n_rows, D_MODEL), jnp.float32),
        compiler_params=pltpu.CompilerParams(
            dimension_semantics=("parallel",), vmem_limit_bytes=V7X_VMEM_LIMIT_BYTES),
        name="post",
    )(ua, ua, ua, za, b_in, ma, mb, x2d, p2d, *weights)


def _layer(x, p, norm_pre, w_in, pool_w, pool_scale, w_branch_a, q_gain, k_gain, w_branch_b, w_out,
           norm_post, ple_norm, w_ple_gate, w_ple_in):
    batch, seq_len, _ = x.shape
    tiles = _tiles(seq_len)
    x2d = x.reshape(batch * seq_len, D_MODEL)
    p2d = p.reshape(batch * seq_len, PLE_DIM)
    ua, za, q, k, v, zb, ma, mb = _in_proj(x2d, seq_len, norm_pre, w_in, q_gain, k_gain, tiles)
    b_in = _attention(q, k, v, zb, batch, seq_len, tiles)
    weights = (pool_w, pool_scale, w_branch_a, w_branch_b, w_out, norm_post, ple_norm, w_ple_gate, w_ple_in)
    y = _post(ua, za, b_in, ma, mb, x2d, p2d, seq_len, weights, tiles)
    return y.reshape(batch, seq_len, D_MODEL)


def kernel(x_prompt, x_sample, p_prompt, p_sample, norm_pre, w_in, pool_w, pool_scale, w_branch_a,
           q_norm, k_norm, w_branch_b, w_out, norm_post, ple_norm, w_ple_gate, w_ple_in):
    depth = w_in.shape[0]
    bf16 = jnp.bfloat16
    for layer in range(depth):
        params = (
            norm_pre[layer][None, :],
            w_in[layer].astype(bf16),
            pool_w[layer].astype(bf16),
            pool_scale[layer][None, :],
            w_branch_a[layer].astype(bf16),
            jnp.tile(q_norm[layer], V7X_LANES // HEAD_DIM)[None, :],
            jnp.tile(k_norm[layer], V7X_LANES // HEAD_DIM)[None, :],
            w_branch_b[layer].astype(bf16),
            w_out[layer].astype(bf16),
            norm_post[layer][None, :],
            ple_norm[layer][None, :],
            w_ple_gate[layer].astype(bf16),
            w_ple_in[layer].astype(bf16),
        )
        x_prompt = _layer(x_prompt, p_prompt[layer], *params)
        x_sample = _layer(x_sample, p_sample[layer], *params)
    return (x_prompt, x_sample)
```

```python
import functools
import math

import jax
import jax.numpy as jnp
from jax import lax
from jax.experimental import pallas as pl
from jax.experimental.pallas import tpu as pltpu

D_MODEL = 1024
GRID_W = 64
PLE_DIM = 256
POOL_WIDTH = 1024
POOL_GROUPS = 4
POOL_GROUP_WIDTH = POOL_WIDTH // POOL_GROUPS
POOL_WINDOWS = (2, 4, 8, 16)
N_Q_HEADS = 16
N_KV_HEADS = 4
HEADS_PER_KV = N_Q_HEADS // N_KV_HEADS
HEAD_DIM = 64
Q_WIDTH = N_Q_HEADS * HEAD_DIM
KV_WIDTH = N_KV_HEADS * HEAD_DIM
ROPE_AXIS_DIM = HEAD_DIM // 2
ROPE_HALF = ROPE_AXIS_DIM // 2
ROPE_BASE = 10000.0
EPS = 1e-6
IN_SPLITS = (POOL_WIDTH, POOL_WIDTH, Q_WIDTH, KV_WIDTH, KV_WIDTH, Q_WIDTH, D_MODEL, D_MODEL)
IN_WIDTH = sum(IN_SPLITS)
IN_OFFSETS = tuple(sum(IN_SPLITS[:i]) for i in range(len(IN_SPLITS)))

V7X_LANES = 128
V7X_SUBLANES = 8
V7X_VMEM_LIMIT_BYTES = 56 * 1024 * 1024

SCORE_SCALE_LOG2 = (HEAD_DIM ** -0.5) * math.log2(math.e)

POOL_HALO = 8
POST_SUBTILES = 2


def _tiles(seq_len):
    assert seq_len % 256 == 0
    return dict(proj_rows=256, post_rows=POST_SUBTILES * 256, q_tile=256, kv_tile=256)


def _rope_tables(seq_len, scale):
    t = jnp.arange(seq_len)
    pos_row = (t // GRID_W).astype(jnp.float32)
    pos_col = (t % GRID_W).astype(jnp.float32)
    freqs = ROPE_BASE ** (-jnp.arange(0, ROPE_AXIS_DIM, 2, dtype=jnp.float32) / ROPE_AXIS_DIM)
    ang_row = pos_row[:, None] * freqs[None, :]
    ang_col = pos_col[:, None] * freqs[None, :]
    zeros = jnp.zeros_like(ang_row)

    def head(fn_row, fn_col):
        return jnp.concatenate([fn_row, fn_row, fn_col, fn_col], axis=-1)

    cos = head(jnp.cos(ang_row), jnp.cos(ang_col))
    sin_lo = jnp.concatenate([zeros, jnp.sin(ang_row), zeros, jnp.sin(ang_col)], axis=-1)
    sin_hi = jnp.concatenate([-jnp.sin(ang_row), zeros, -jnp.sin(ang_col), zeros], axis=-1)
    two = lambda a: jnp.concatenate([a, a], axis=-1) * scale
    return two(cos), two(sin_lo), two(sin_hi)


def _head_norm_rope(x, gain, cos, sin_lo, sin_hi):
    lane = lax.broadcasted_iota(jnp.int32, x.shape, 1)
    first = lane < HEAD_DIM
    x2 = x * x
    ss_a = jnp.sum(jnp.where(first, x2, 0.0), axis=-1, keepdims=True)
    ss_b = jnp.sum(jnp.where(first, 0.0, x2), axis=-1, keepdims=True)
    ms = jnp.where(first, ss_a, ss_b) * (1.0 / HEAD_DIM)
    xn = x * lax.rsqrt(ms + EPS) * gain
    return (xn * cos
            + pltpu.roll(xn, ROPE_HALF, 1) * sin_lo
            + pltpu.roll(xn, V7X_LANES - ROPE_HALF, 1) * sin_hi)


def _in_proj_kernel(x_ref, g_ref, w_ref, qg_ref, kg_ref, cq_ref, slq_ref, shq_ref, ck_ref, slk_ref, shk_ref,
                    ua_ref, za_ref, q_ref, k_ref, v_ref, zb_ref, ma_ref, mb_ref):
    x = x_ref[...]
    h = x * lax.rsqrt(jnp.mean(x * x, axis=-1, keepdims=True) + EPS) * g_ref[...]
    h = h.astype(jnp.bfloat16)

    def proj(idx):
        lo = IN_OFFSETS[idx]
        return jnp.dot(h, w_ref[:, lo:lo + IN_SPLITS[idx]], preferred_element_type=jnp.float32)

    ua_ref[...] = proj(0)
    za_ref[...] = proj(1).astype(za_ref.dtype)

    q = proj(2)
    cq, slq, shq = cq_ref[...], slq_ref[...], shq_ref[...]
    for c in range(Q_WIDTH // V7X_LANES):
        sl = slice(c * V7X_LANES, (c + 1) * V7X_LANES)
        q_ref[0, sl, :] = _head_norm_rope(q[:, sl], qg_ref[...], cq, slq, shq).T.astype(q_ref.dtype)

    k = proj(3)
    ck, slk, shk = ck_ref[...], slk_ref[...], shk_ref[...]
    heads_per_vreg = V7X_LANES // HEAD_DIM
    for c in range(KV_WIDTH // V7X_LANES):
        sl = slice(c * V7X_LANES, (c + 1) * V7X_LANES)
        k_c = _head_norm_rope(k[:, sl], kg_ref[...], ck, slk, shk).astype(k_ref.dtype)
        for j in range(heads_per_vreg):
            k_ref[0, c * heads_per_vreg + j] = k_c[:, j * HEAD_DIM:(j + 1) * HEAD_DIM]

    v_t = proj(4).T
    for g in range(N_KV_HEADS):
        v_ref[0, g, 0] = v_t[g * HEAD_DIM:(g + 1) * HEAD_DIM].astype(v_ref.dtype)
    zb_ref[...] = proj(5).astype(zb_ref.dtype)
    ma_ref[...] = proj(6).astype(ma_ref.dtype)
    mb_ref[...] = proj(7).astype(mb_ref.dtype)


def _in_proj(x2d, seq_len, norm_pre, w_in_bf16, q_gain, k_gain, tiles):
    n_rows = x2d.shape[0]
    batch = n_rows // seq_len
    tm = tiles["proj_rows"]
    assert tm == tiles["kv_tile"]
    seq_tiles = seq_len // tm
    cq, slq, shq = _rope_tables(seq_len, SCORE_SCALE_LOG2)
    ck, slk, shk = _rope_tables(seq_len, 1.0)

    row = lambda width: pl.BlockSpec((tm, width), lambda i: (i, 0))
    const = lambda shape: pl.BlockSpec(shape, lambda i: (0, 0))
    table = pl.BlockSpec((tm, V7X_LANES), lambda i: (i % seq_tiles, 0))
    bf16 = jnp.bfloat16
    out_shapes = (
        jax.ShapeDtypeStruct((n_rows, POOL_WIDTH), jnp.float32),
        jax.ShapeDtypeStruct((n_rows, POOL_WIDTH), bf16),
        jax.ShapeDtypeStruct((batch, Q_WIDTH, seq_len), bf16),
        jax.ShapeDtypeStruct((batch, N_KV_HEADS, seq_len, HEAD_DIM), bf16),
        jax.ShapeDtypeStruct((batch, N_KV_HEADS, seq_tiles, HEAD_DIM, tm), bf16),
        jax.ShapeDtypeStruct((n_rows, Q_WIDTH), bf16),
        jax.ShapeDtypeStruct((n_rows, D_MODEL), bf16),
        jax.ShapeDtypeStruct((n_rows, D_MODEL), bf16),
    )
    out_specs = (
        row(POOL_WIDTH), row(POOL_WIDTH),
        pl.BlockSpec((1, Q_WIDTH, tm), lambda i: (i // seq_tiles, 0, i % seq_tiles)),
        pl.BlockSpec((1, N_KV_HEADS, tm, HEAD_DIM), lambda i: (i // seq_tiles, 0, i % seq_tiles, 0)),
        pl.BlockSpec((1, N_KV_HEADS, 1, HEAD_DIM, tm), lambda i: (i // seq_tiles, 0, i % seq_tiles, 0, 0)),
        row(Q_WIDTH), row(D_MODEL), row(D_MODEL),
    )
    return pl.pallas_call(
        _in_proj_kernel,
        grid=(n_rows // tm,),
        in_specs=[row(D_MODEL), const((1, D_MODEL)), const((D_MODEL, IN_WIDTH)),
                  const((1, V7X_LANES)), const((1, V7X_LANES)),
                  table, table, table, table, table, table],
        out_specs=out_specs,
        out_shape=out_shapes,
        compiler_params=pltpu.CompilerParams(
            dimension_semantics=("parallel",), vmem_limit_bytes=V7X_VMEM_LIMIT_BYTES),
        name="in_proj",
    )(x2d, norm_pre, w_in_bf16, q_gain, k_gain, cq, slq, shq, ck, slk, shk)


ACC_ROWS = HEAD_DIM + 16
CHUNKS_PER_ITER = 8
PIPE_SLOTS = 2


def _attention_kernel(q_ref, k_ref, vt_ref, zb_ref, o_ref, *scratch, n_chunks, kv_tile):
    s_refs, p_refs = scratch[:PIPE_SLOTS], scratch[PIPE_SLOTS:2 * PIPE_SLOTS]
    acc_refs = scratch[2 * PIPE_SLOTS:]
    tq = q_ref.shape[3]
    heads = range(HEADS_PER_KV)
    ones_rows = (lax.broadcasted_iota(jnp.int32, (ACC_ROWS - HEAD_DIM, kv_tile), 0) == 0).astype(jnp.bfloat16)

    def stage1(c, s_ref):
        start = pl.multiple_of(c * kv_tile, kv_tile)
        k_c = k_ref[0, 0, pl.ds(start, kv_tile), :]
        tile_max = []
        for hq in heads:
            s_t = jnp.dot(k_c, q_ref[0, hq], preferred_element_type=jnp.float32)
            s_ref[hq] = s_t
            m8 = s_t[:V7X_SUBLANES]
            for r in range(1, kv_tile // V7X_SUBLANES):
                m8 = jnp.maximum(m8, s_t[r * V7X_SUBLANES:(r + 1) * V7X_SUBLANES])
            tile_max.append(jnp.max(m8, axis=0, keepdims=True))
        return tuple(tile_max)

    def stage2(s_ref, tile_max, m_run, p_ref):
        m_out, alpha = [], []
        for hq in heads:
            m_new = jnp.maximum(m_run[hq], tile_max[hq])
            alpha.append(jnp.exp2(m_run[hq] - m_new))
            p_ref[hq] = jnp.exp2(s_ref[hq] - m_new).astype(jnp.bfloat16)
            m_out.append(m_new)
        return tuple(m_out), tuple(alpha)

    def stage3(c, p_ref, alpha):
        lhs = jnp.concatenate([vt_ref[0, 0, c], ones_rows], axis=0)
        for hq in heads:
            pv = jnp.dot(lhs, p_ref[hq], preferred_element_type=jnp.float32)
            acc_refs[hq][...] = alpha[hq] * acc_refs[hq][...] + pv

    for acc_ref in acc_refs:
        acc_ref[...] = jnp.zeros(acc_ref.shape, jnp.float32)
    prev_slot = PIPE_SLOTS - 1
    p_refs[prev_slot][...] = jnp.zeros(p_refs[prev_slot].shape, jnp.bfloat16)
    last = n_chunks - 1

    def body(j, carry):
        m_run, tile_max, alpha_prev = carry
        base = CHUNKS_PER_ITER * j
        for u in range(CHUNKS_PER_ITER):
            prv, cur, nxt = (u - 1) % PIPE_SLOTS, u % PIPE_SLOTS, (u + 1) % PIPE_SLOTS
            c = base + u
            stage3(jnp.maximum(c - 1, 0), p_refs[prv], alpha_prev)
            m_run, alpha_prev = stage2(s_refs[cur], tile_max, m_run, p_refs[cur])
            tile_max = stage1(jnp.minimum(c + 1, last), s_refs[nxt])
        return m_run, tile_max, alpha_prev

    init = (tuple(jnp.full((1, tq), -jnp.inf, jnp.float32) for _ in heads),
            stage1(0, s_refs[0]),
            tuple(jnp.ones((1, tq), jnp.float32) for _ in heads))
    _, _, alpha_last = lax.fori_loop(0, n_chunks // CHUNKS_PER_ITER, body, init)
    stage3(last, p_refs[prev_slot], alpha_last)

    parts = []
    for hq in heads:
        acc = acc_refs[hq][...]
        parts.append(acc[:HEAD_DIM] / acc[HEAD_DIM:HEAD_DIM + 1])
    o_t = jnp.concatenate(parts, axis=0)
    zb = zb_ref[...].astype(jnp.float32)
    o_ref[...] = (o_t.T * (zb * jax.nn.sigmoid(zb))).astype(o_ref.dtype)


def _attention_bounded_kernel(bound_ref, q_ref, k_ref, vt_ref, zb_ref, o_ref, *scratch,
                              n_chunks, kv_tile, unroll):
    p_refs, acc_refs = scratch[:2], scratch[2:]
    heads = range(HEADS_PER_KV)
    ones_rows = (lax.broadcasted_iota(jnp.int32, (ACC_ROWS - HEAD_DIM, kv_tile), 0) == 0).astype(jnp.bfloat16)
    bound = bound_ref[...]

    def stage_a(c, p_ref):
        start = pl.multiple_of(c * kv_tile, kv_tile)
        k_c = k_ref[0, 0, pl.ds(start, kv_tile), :]
        for hq in heads:
            s_t = jnp.dot(k_c, q_ref[0, hq], preferred_element_type=jnp.float32)
            p_ref[hq] = jnp.exp2(s_t - bound).astype(jnp.bfloat16)

    def stage_b(c, p_ref):
        lhs = jnp.concatenate([vt_ref[0, 0, c], ones_rows], axis=0)
        for hq in heads:
            acc_refs[hq][...] += jnp.dot(lhs, p_ref[hq], preferred_element_type=jnp.float32)

    for acc_ref in acc_refs:
        acc_ref[...] = jnp.zeros(acc_ref.shape, jnp.float32)
    stage_a(0, p_refs[0])
    last = n_chunks - 1

    def body(j, carry):
        base = unroll * j
        for u in range(unroll):
            c = base + u
            stage_a(jnp.minimum(c + 1, last), p_refs[(u + 1) % 2])
            stage_b(c, p_refs[u % 2])
        return carry

    if unroll == n_chunks:
        for c in range(n_chunks):
            if c + 1 < n_chunks:
                stage_a(c + 1, p_refs[(c + 1) % 2])
            stage_b(c, p_refs[c % 2])
    else:
        lax.fori_loop(0, n_chunks // unroll, body, 0)

    parts = []
    for hq in heads:
        acc = acc_refs[hq][...]
        parts.append(acc[:HEAD_DIM] / acc[HEAD_DIM:HEAD_DIM + 1])
    o_t = jnp.concatenate(parts, axis=0)
    zb = zb_ref[...].astype(jnp.float32)
    o_ref[...] = (o_t.T * (zb * jax.nn.sigmoid(zb))).astype(o_ref.dtype)


BOUNDED_SOFTMAX_LIMIT = 40.0
BOUNDED_CHUNKS_PER_ITER = 32


def _attention(q_t, k_h, v_t, zb, score_bound, batch, seq_len, tiles):
    tq, tk = tiles["q_tile"], tiles["kv_tile"]
    n_chunks = seq_len // tk
    assert n_chunks % CHUNKS_PER_ITER == 0 and CHUNKS_PER_ITER % PIPE_SLOTS == 0
    q_tiles = seq_len // tq
    q_h = q_t.reshape(batch, N_Q_HEADS, HEAD_DIM, seq_len)

    group_width = HEADS_PER_KV * HEAD_DIM
    score_buf = lambda dtype: pltpu.VMEM((HEADS_PER_KV, tk, tq), dtype)
    acc_bufs = [pltpu.VMEM((ACC_ROWS, tq), jnp.float32) for _ in range(HEADS_PER_KV)]
    in_specs = [
        pl.BlockSpec((1, HEADS_PER_KV, HEAD_DIM, tq), lambda b, g, i: (b, g, 0, i)),
        pl.BlockSpec((1, 1, seq_len, HEAD_DIM), lambda b, g, i: (b, g, 0, 0)),
        pl.BlockSpec((1, 1, n_chunks, HEAD_DIM, tk), lambda b, g, i: (b, g, 0, 0, 0)),
        pl.BlockSpec((tq, group_width), lambda b, g, i: (b * q_tiles + i, g)),
    ]
    common = dict(
        grid=(batch, N_KV_HEADS, q_tiles),
        out_specs=pl.BlockSpec((tq, group_width), lambda b, g, i: (b * q_tiles + i, g)),
        out_shape=jax.ShapeDtypeStruct((batch * seq_len, Q_WIDTH), jnp.bfloat16),
        compiler_params=pltpu.CompilerParams(
            dimension_semantics=("parallel", "parallel", "parallel"), vmem_limit_bytes=V7X_VMEM_LIMIT_BYTES),
    )

    def bounded(bound, *operands):
        return pl.pallas_call(
            functools.partial(_attention_bounded_kernel, n_chunks=n_chunks, kv_tile=tk,
                              unroll=math.gcd(n_chunks, BOUNDED_CHUNKS_PER_ITER)),
            in_specs=[pl.BlockSpec((1, 1), lambda b, g, i: (0, 0))] + in_specs,
            scratch_shapes=[score_buf(jnp.bfloat16) for _ in range(2)] + acc_bufs,
            name="attention_bounded", **common,
        )(bound.reshape(1, 1), *operands)

    def general(bound, *operands):
        return pl.pallas_call(
            functools.partial(_attention_kernel, n_chunks=n_chunks, kv_tile=tk),
            in_specs=in_specs,
            scratch_shapes=[score_buf(jnp.float32) for _ in range(PIPE_SLOTS)]
                           + [score_buf(jnp.bfloat16) for _ in range(PIPE_SLOTS)] + acc_bufs,
            name="attention", **common,
        )(*operands)

    return lax.cond(score_bound <= BOUNDED_SOFTMAX_LIMIT, bounded, general, score_bound, q_h, k_h, v_t, zb)


def _window_sums(ext, window):
    n = ext.shape[0]
    back = lambda a, s: pltpu.roll(a, s, 0)
    fwd = lambda a, s: pltpu.roll(a, n - s, 0)
    half = window // 2
    run, span = ext, 1
    while span < half:
        run = run + back(run, span)
        span *= 2
    ahead = fwd(run, half - 1) if half > 1 else run
    total = ahead + back(run, 1)
    return total[POOL_HALO:n - POOL_HALO]


def _post_kernel(ua_ref, prev_ref, next_ref, za_ref, b_ref, ma_ref, mb_ref, x_ref, p_ref,
                 pool_w_ref, pool_scale_ref, wa_ref, wb_ref, wo_ref, npost_ref, nple_ref, wg_ref, wp_ref,
                 y_ref, *, seq_len, rows):
    i = pl.program_id(0)
    tiles_per_seq = seq_len // rows
    t0 = (i % tiles_per_seq) * rows
    f32, bf16 = jnp.float32, jnp.bfloat16
    dot = functools.partial(jnp.dot, preferred_element_type=f32)
    sub = rows // POST_SUBTILES
    parts = [slice(h * sub, (h + 1) * sub) for h in range(POST_SUBTILES)]

    b = [dot(b_ref[rs, :], wb_ref[...]) for rs in parts]

    ua = ua_ref[...]
    prev = jnp.where(t0 == 0, 0.0, prev_ref[...])
    nxt = jnp.where(t0 + rows == seq_len, 0.0, next_ref[...])
    ext = jnp.concatenate([prev, ua, nxt], axis=0)
    t = t0 + lax.broadcasted_iota(jnp.int32, (rows, 1), 0)
    pooled = []
    for gi, w in enumerate(POOL_WINDOWS):
        cols = slice(gi * POOL_GROUP_WIDTH, (gi + 1) * POOL_GROUP_WIDTH)
        total = _window_sums(ext[:, cols], w)
        lo = jnp.maximum(t - w // 2, 0)
        hi = jnp.minimum(t + (w - w // 2), seq_len)
        cnt = (hi - lo).astype(f32)
        pooled.append((total / cnt - ua[:, cols]).astype(bf16))
    mixed = [[dot(pooled[gi][rs], pool_w_ref[gi]) for gi in range(POOL_GROUPS)] for rs in parts]

    pe = [dot(p_ref[rs, :].astype(bf16), wp_ref[...]) for rs in parts]

    a = []
    for rs, mixed_h in zip(parts, mixed):
        za = za_ref[rs, :].astype(f32)
        a_in = jnp.concatenate(mixed_h, axis=1) * pool_scale_ref[...] * (za * jax.nn.sigmoid(za))
        a.append(dot(a_in.astype(bf16), wa_ref[...]))

    o = []
    for rs, a_h, b_h in zip(parts, a, b):
        ma = ma_ref[rs, :].astype(f32)
        mb = mb_ref[rs, :].astype(f32)
        merged = jax.nn.sigmoid(ma) * a_h + jax.nn.sigmoid(mb) * b_h
        o.append(dot(merged.astype(bf16), wo_ref[...]))

    x1, gate = [], []
    for rs, o_h in zip(parts, o):
        o_n = o_h * lax.rsqrt(jnp.mean(o_h * o_h, axis=-1, keepdims=True) + EPS) * npost_ref[...]
        x1_h = x_ref[rs, :] + o_n
        xn = x1_h * lax.rsqrt(jnp.mean(x1_h * x1_h, axis=-1, keepdims=True) + EPS) * nple_ref[...]
        x1.append(x1_h)
        gate.append(dot(xn.astype(bf16), wg_ref[...]))

    for rs, x1_h, gate_h, pe_h in zip(parts, x1, gate, pe):
        y_ref[rs, :] = x1_h + jax.nn.sigmoid(gate_h) * pe_h


def _post(ua, za, b_in, ma, mb, x2d, p2d, seq_len, weights, tiles):
    n_rows = x2d.shape[0]
    tm = tiles["post_rows"]
    halo_blocks = tm // POOL_HALO
    last_halo = n_rows // POOL_HALO - 1
    row = lambda width: pl.BlockSpec((tm, width), lambda i: (i, 0))
    const2 = lambda shape: pl.BlockSpec(shape, lambda i: (0, 0), pipeline_mode=pl.Buffered(1))
    kernel = functools.partial(_post_kernel, seq_len=seq_len, rows=tm)
    return pl.pallas_call(
        kernel,
        grid=(n_rows // tm,),
        in_specs=[
            row(POOL_WIDTH),
            pl.BlockSpec((POOL_HALO, POOL_WIDTH), lambda i: (jnp.maximum(i * halo_blocks - 1, 0), 0)),
            pl.BlockSpec((POOL_HALO, POOL_WIDTH), lambda i: (jnp.minimum((i + 1) * halo_blocks, last_halo), 0)),
            row(POOL_WIDTH), row(Q_WIDTH), row(D_MODEL), row(D_MODEL), row(D_MODEL), row(PLE_DIM),
            pl.BlockSpec((POOL_GROUPS, POOL_GROUP_WIDTH, POOL_GROUP_WIDTH), lambda i: (0, 0, 0),
                         pipeline_mode=pl.Buffered(1)),
            const2((1, POOL_WIDTH)),
            const2((POOL_WIDTH, D_MODEL)), const2((Q_WIDTH, D_MODEL)), const2((D_MODEL, D_MODEL)),
            const2((1, D_MODEL)), const2((1, D_MODEL)),
            const2((D_MODEL, D_MODEL)), const2((PLE_DIM, D_MODEL)),
        ],
        out_specs=row(D_MODEL),
        out_shape=jax.ShapeDtypeStruct((n_rows, D_MODEL), jnp.float32),
        compiler_params=pltpu.CompilerParams(
            dimension_semantics=("parallel",), vmem_limit_bytes=V7X_VMEM_LIMIT_BYTES),
        name="post",
    )(ua, ua, ua, za, b_in, ma, mb, x2d, p2d, *weights)


def _layer(x, p, norm_pre, w_in, pool_w, pool_scale, w_branch_a, q_gain, k_gain, w_branch_b, w_out,
           norm_post, ple_norm, w_ple_gate, w_ple_in):
    batch, seq_len, _ = x.shape
    tiles = _tiles(seq_len)
    x2d = x.reshape(batch * seq_len, D_MODEL)
    p2d = p.reshape(batch * seq_len, PLE_DIM)
    ua, za, q, k, v, zb, ma, mb = _in_proj(x2d, seq_len, norm_pre, w_in, q_gain, k_gain, tiles)
    score_bound = (HEAD_DIM * SCORE_SCALE_LOG2 * 1.02) * jnp.max(jnp.abs(q_gain)) * jnp.max(jnp.abs(k_gain))
    b_in = _attention(q, k, v, zb, score_bound, batch, seq_len, tiles)
    weights = (pool_w, pool_scale, w_branch_a, w_branch_b, w_out, norm_post, ple_norm, w_ple_gate, w_ple_in)
    y = _post(ua, za, b_in, ma, mb, x2d, p2d, seq_len, weights, tiles)
    return y.reshape(batch, seq_len, D_MODEL)


def kernel(x_prompt, x_sample, p_prompt, p_sample, norm_pre, w_in, pool_w, pool_scale, w_branch_a,
           q_norm, k_norm, w_branch_b, w_out, norm_post, ple_norm, w_ple_gate, w_ple_in):
    depth = w_in.shape[0]
    bf16 = jnp.bfloat16
    for layer in range(depth):
        params = (
            norm_pre[layer][None, :],
            w_in[layer].astype(bf16),
            pool_w[layer].astype(bf16),
            pool_scale[layer][None, :],
            w_branch_a[layer].astype(bf16),
            jnp.tile(q_norm[layer], V7X_LANES // HEAD_DIM)[None, :],
            jnp.tile(k_norm[layer], V7X_LANES // HEAD_DIM)[None, :],
            w_branch_b[layer].astype(bf16),
            w_out[layer].astype(bf16),
            norm_post[layer][None, :],
            ple_norm[layer][None, :],
            w_ple_gate[layer].astype(bf16),
            w_ple_in[layer].astype(bf16),
        )
        x_prompt = _layer(x_prompt, p_prompt[layer], *params)
        x_sample = _layer(x_sample, p_sample[layer], *params)
    return (x_prompt, x_sample)
```

```python
import functools
import math

import jax
import jax.numpy as jnp
from jax import lax
from jax.experimental import pallas as pl
from jax.experimental.pallas import tpu as pltpu

D_MODEL = 1024
GRID_W = 64
PLE_DIM = 256
POOL_WIDTH = 1024
POOL_GROUPS = 4
POOL_GROUP_WIDTH = POOL_WIDTH // POOL_GROUPS
POOL_WINDOWS = (2, 4, 8, 16)
N_Q_HEADS = 16
N_KV_HEADS = 4
HEADS_PER_KV = N_Q_HEADS // N_KV_HEADS
HEAD_DIM = 64
Q_WIDTH = N_Q_HEADS * HEAD_DIM
KV_WIDTH = N_KV_HEADS * HEAD_DIM
ROPE_AXIS_DIM = HEAD_DIM // 2
ROPE_HALF = ROPE_AXIS_DIM // 2
ROPE_BASE = 10000.0
EPS = 1e-6
IN_SPLITS = (POOL_WIDTH, POOL_WIDTH, Q_WIDTH, KV_WIDTH, KV_WIDTH, Q_WIDTH, D_MODEL, D_MODEL)
IN_WIDTH = sum(IN_SPLITS)
IN_OFFSETS = tuple(sum(IN_SPLITS[:i]) for i in range(len(IN_SPLITS)))

V7X_LANES = 128
V7X_SUBLANES = 8
V7X_VMEM_LIMIT_BYTES = 56 * 1024 * 1024

SCORE_SCALE_LOG2 = (HEAD_DIM ** -0.5) * math.log2(math.e)

POOL_HALO = 8
POST_SUBTILES = 2


def _tiles(seq_len):
    assert seq_len % 256 == 0
    return dict(proj_rows=256, post_rows=POST_SUBTILES * 256, q_tile=256, kv_tile=256)


def _rope_tables(seq_len, scale):
    t = jnp.arange(seq_len)
    pos_row = (t // GRID_W).astype(jnp.float32)
    pos_col = (t % GRID_W).astype(jnp.float32)
    freqs = ROPE_BASE ** (-jnp.arange(0, ROPE_AXIS_DIM, 2, dtype=jnp.float32) / ROPE_AXIS_DIM)
    ang_row = pos_row[:, None] * freqs[None, :]
    ang_col = pos_col[:, None] * freqs[None, :]
    zeros = jnp.zeros_like(ang_row)

    def head(fn_row, fn_col):
        return jnp.concatenate([fn_row, fn_row, fn_col, fn_col], axis=-1)

    cos = head(jnp.cos(ang_row), jnp.cos(ang_col))
    sin_lo = jnp.concatenate([zeros, jnp.sin(ang_row), zeros, jnp.sin(ang_col)], axis=-1)
    sin_hi = jnp.concatenate([-jnp.sin(ang_row), zeros, -jnp.sin(ang_col), zeros], axis=-1)
    two = lambda a: jnp.concatenate([a, a], axis=-1) * scale
    return two(cos), two(sin_lo), two(sin_hi)


def _head_norm_rope(x, gain, cos, sin_lo, sin_hi):
    lane = lax.broadcasted_iota(jnp.int32, x.shape, 1)
    first = lane < HEAD_DIM
    x2 = x * x
    ss_a = jnp.sum(jnp.where(first, x2, 0.0), axis=-1, keepdims=True)
    ss_b = jnp.sum(jnp.where(first, 0.0, x2), axis=-1, keepdims=True)
    ms = jnp.where(first, ss_a, ss_b) * (1.0 / HEAD_DIM)
    xn = x * lax.rsqrt(ms + EPS) * gain
    return (xn * cos
            + pltpu.roll(xn, ROPE_HALF, 1) * sin_lo
            + pltpu.roll(xn, V7X_LANES - ROPE_HALF, 1) * sin_hi)


def _in_proj_kernel(x_ref, g_ref, w_ref, qg_ref, kg_ref, cq_ref, slq_ref, shq_ref, ck_ref, slk_ref, shk_ref,
                    ua_ref, za_ref, q_ref, k_ref, v_ref, zb_ref, ma_ref, mb_ref):
    x = x_ref[...]
    h = x * lax.rsqrt(jnp.mean(x * x, axis=-1, keepdims=True) + EPS) * g_ref[...]
    h = h.astype(jnp.bfloat16)

    def proj(idx):
        lo = IN_OFFSETS[idx]
        return jnp.dot(h, w_ref[:, lo:lo + IN_SPLITS[idx]], preferred_element_type=jnp.float32)

    ua_ref[...] = proj(0)
    za_ref[...] = proj(1).astype(za_ref.dtype)

    q = proj(2)
    cq, slq, shq = cq_ref[...], slq_ref[...], shq_ref[...]
    for c in range(Q_WIDTH // V7X_LANES):
        sl = slice(c * V7X_LANES, (c + 1) * V7X_LANES)
        q_ref[0, sl, :] = _head_norm_rope(q[:, sl], qg_ref[...], cq, slq, shq).T.astype(q_ref.dtype)

    k = proj(3)
    ck, slk, shk = ck_ref[...], slk_ref[...], shk_ref[...]
    heads_per_vreg = V7X_LANES // HEAD_DIM
    for c in range(KV_WIDTH // V7X_LANES):
        sl = slice(c * V7X_LANES, (c + 1) * V7X_LANES)
        k_c = _head_norm_rope(k[:, sl], kg_ref[...], ck, slk, shk).astype(k_ref.dtype)
        for j in range(heads_per_vreg):
            k_ref[0, c * heads_per_vreg + j] = k_c[:, j * HEAD_DIM:(j + 1) * HEAD_DIM]

    v_t = proj(4).T
    for g in range(N_KV_HEADS):
        v_ref[0, g, 0] = v_t[g * HEAD_DIM:(g + 1) * HEAD_DIM].astype(v_ref.dtype)
    zb_ref[...] = proj(5).astype(zb_ref.dtype)
    ma_ref[...] = proj(6).astype(ma_ref.dtype)
    mb_ref[...] = proj(7).astype(mb_ref.dtype)


def _in_proj(x2d, seq_len, norm_pre, w_in_bf16, q_gain, k_gain, tiles):
    n_rows = x2d.shape[0]
    batch = n_rows // seq_len
    tm = tiles["proj_rows"]
    assert tm == tiles["kv_tile"]
    seq_tiles = seq_len // tm
    cq, slq, shq = _rope_tables(seq_len, SCORE_SCALE_LOG2)
    ck, slk, shk = _rope_tables(seq_len, 1.0)

    row = lambda width: pl.BlockSpec((tm, width), lambda i: (i, 0))
    const = lambda shape: pl.BlockSpec(shape, lambda i: (0, 0))
    table = pl.BlockSpec((tm, V7X_LANES), lambda i: (i % seq_tiles, 0))
    bf16 = jnp.bfloat16
    out_shapes = (
        jax.ShapeDtypeStruct((n_rows, POOL_WIDTH), jnp.float32),
        jax.ShapeDtypeStruct((n_rows, POOL_WIDTH), bf16),
        jax.ShapeDtypeStruct((batch, Q_WIDTH, seq_len), bf16),
        jax.ShapeDtypeStruct((batch, N_KV_HEADS, seq_len, HEAD_DIM), bf16),
        jax.ShapeDtypeStruct((batch, N_KV_HEADS, seq_tiles, HEAD_DIM, tm), bf16),
        jax.ShapeDtypeStruct((n_rows, Q_WIDTH), bf16),
        jax.ShapeDtypeStruct((n_rows, D_MODEL), bf16),
        jax.ShapeDtypeStruct((n_rows, D_MODEL), bf16),
    )
    out_specs = (
        row(POOL_WIDTH), row(POOL_WIDTH),
        pl.BlockSpec((1, Q_WIDTH, tm), lambda i: (i // seq_tiles, 0, i % seq_tiles)),
        pl.BlockSpec((1, N_KV_HEADS, tm, HEAD_DIM), lambda i: (i // seq_tiles, 0, i % seq_tiles, 0)),
        pl.BlockSpec((1, N_KV_HEADS, 1, HEAD_DIM, tm), lambda i: (i // seq_tiles, 0, i % seq_tiles, 0, 0)),
        row(Q_WIDTH), row(D_MODEL), row(D_MODEL),
    )
    return pl.pallas_call(
        _in_proj_kernel,
        grid=(n_rows // tm,),
        in_specs=[row(D_MODEL), const((1, D_MODEL)), const((D_MODEL, IN_WIDTH)),
                  const((1, V7X_LANES)), const((1, V7X_LANES)),
                  table, table, table, table, table, table],
        out_specs=out_specs,
        out_shape=out_shapes,
        compiler_params=pltpu.CompilerParams(
            dimension_semantics=("parallel",), vmem_limit_bytes=V7X_VMEM_LIMIT_BYTES),
        name="in_proj",
    )(x2d, norm_pre, w_in_bf16, q_gain, k_gain, cq, slq, shq, ck, slk, shk)


ACC_ROWS = HEAD_DIM + 16
CHUNKS_PER_ITER = 8
PIPE_SLOTS = 2


def _attention_kernel(q_ref, k_ref, vt_ref, zb_ref, o_ref, *scratch, n_chunks, kv_tile):
    s_refs, p_refs = scratch[:PIPE_SLOTS], scratch[PIPE_SLOTS:2 * PIPE_SLOTS]
    acc_refs = scratch[2 * PIPE_SLOTS:]
    tq = q_ref.shape[3]
    heads = range(HEADS_PER_KV)
    ones_rows = (lax.broadcasted_iota(jnp.int32, (ACC_ROWS - HEAD_DIM, kv_tile), 0) == 0).astype(jnp.bfloat16)

    def stage1(c, s_ref):
        start = pl.multiple_of(c * kv_tile, kv_tile)
        k_c = k_ref[0, 0, pl.ds(start, kv_tile), :]
        tile_max = []
        for hq in heads:
            s_t = jnp.dot(k_c, q_ref[0, hq], preferred_element_type=jnp.float32)
            s_ref[hq] = s_t
            m8 = s_t[:V7X_SUBLANES]
            for r in range(1, kv_tile // V7X_SUBLANES):
                m8 = jnp.maximum(m8, s_t[r * V7X_SUBLANES:(r + 1) * V7X_SUBLANES])
            tile_max.append(jnp.max(m8, axis=0, keepdims=True))
        return tuple(tile_max)

    def stage2(s_ref, tile_max, m_run, p_ref):
        m_out, alpha = [], []
        for hq in heads:
            m_new = jnp.maximum(m_run[hq], tile_max[hq])
            alpha.append(jnp.exp2(m_run[hq] - m_new))
            p_ref[hq] = jnp.exp2(s_ref[hq] - m_new).astype(jnp.bfloat16)
            m_out.append(m_new)
        return tuple(m_out), tuple(alpha)

    def stage3(c, p_ref, alpha):
        lhs = jnp.concatenate([vt_ref[0, 0, c], ones_rows], axis=0)
        for hq in heads:
            pv = jnp.dot(lhs, p_ref[hq], preferred_element_type=jnp.float32)
            acc_refs[hq][...] = alpha[hq] * acc_refs[hq][...] + pv

    for acc_ref in acc_refs:
        acc_ref[...] = jnp.zeros(acc_ref.shape, jnp.float32)
    prev_slot = PIPE_SLOTS - 1
    p_refs[prev_slot][...] = jnp.zeros(p_refs[prev_slot].shape, jnp.bfloat16)
    last = n_chunks - 1

    def body(j, carry):
        m_run, tile_max, alpha_prev = carry
        base = CHUNKS_PER_ITER * j
        for u in range(CHUNKS_PER_ITER):
            prv, cur, nxt = (u - 1) % PIPE_SLOTS, u % PIPE_SLOTS, (u + 1) % PIPE_SLOTS
            c = base + u
            stage3(jnp.maximum(c - 1, 0), p_refs[prv], alpha_prev)
            m_run, alpha_prev = stage2(s_refs[cur], tile_max, m_run, p_refs[cur])
            tile_max = stage1(jnp.minimum(c + 1, last), s_refs[nxt])
        return m_run, tile_max, alpha_prev

    init = (tuple(jnp.full((1, tq), -jnp.inf, jnp.float32) for _ in heads),
            stage1(0, s_refs[0]),
            tuple(jnp.ones((1, tq), jnp.float32) for _ in heads))
    _, _, alpha_last = lax.fori_loop(0, n_chunks // CHUNKS_PER_ITER, body, init)
    stage3(last, p_refs[prev_slot], alpha_last)

    parts = []
    for hq in heads:
        acc = acc_refs[hq][...]
        parts.append(acc[:HEAD_DIM] / acc[HEAD_DIM:HEAD_DIM + 1])
    o_t = jnp.concatenate(parts, axis=0)
    zb = zb_ref[...].astype(jnp.float32)
    o_ref[...] = (o_t.T * (zb * jax.nn.sigmoid(zb))).astype(o_ref.dtype)


def _attention_bounded_kernel(bound_ref, q_ref, k_ref, vt_ref, zb_ref, o_ref, *scratch, n_chunks, kv_tile):
    p_refs, acc_refs = scratch[:2], scratch[2:]
    tq = q_ref.shape[3]
    heads = range(HEADS_PER_KV)
    bound = bound_ref[...]
    row_sums = [jnp.zeros((V7X_SUBLANES, tq), jnp.float32) for _ in heads]

    def stage_a(c, hq, p_ref):
        k_c = k_ref[0, 0, c * kv_tile:(c + 1) * kv_tile, :]
        s_t = jnp.dot(k_c, q_ref[0, hq], preferred_element_type=jnp.float32)
        p_t = jnp.exp2(s_t - bound)
        row_sums[hq] = row_sums[hq] + p_t.reshape(kv_tile // V7X_SUBLANES, V7X_SUBLANES, tq).sum(axis=0)
        p_ref[hq] = p_t.astype(jnp.bfloat16)

    def stage_b(c, hq, p_ref):
        pv = jnp.dot(vt_ref[0, 0, c], p_ref[hq], preferred_element_type=jnp.float32)
        acc_refs[hq][...] = pv if c == 0 else acc_refs[hq][...] + pv

    for hq in heads:
        stage_a(0, hq, p_refs[0])
    for c in range(n_chunks):
        for hq in heads:
            if c + 1 < n_chunks:
                stage_a(c + 1, hq, p_refs[(c + 1) % 2])
            stage_b(c, hq, p_refs[c % 2])

    parts = [acc_refs[hq][...] / jnp.sum(row_sums[hq], axis=0, keepdims=True) for hq in heads]
    o_t = jnp.concatenate(parts, axis=0)
    zb = zb_ref[...].astype(jnp.float32)
    o_ref[...] = (o_t.T * (zb * jax.nn.sigmoid(zb))).astype(o_ref.dtype)


BOUNDED_SOFTMAX_LIMIT = 40.0


def _attention(q_t, k_h, v_t, zb, score_bound, batch, seq_len, tiles):
    tq, tk = tiles["q_tile"], tiles["kv_tile"]
    n_chunks = seq_len // tk
    assert n_chunks % CHUNKS_PER_ITER == 0 and CHUNKS_PER_ITER % PIPE_SLOTS == 0
    q_tiles = seq_len // tq
    q_h = q_t.reshape(batch, N_Q_HEADS, HEAD_DIM, seq_len)

    group_width = HEADS_PER_KV * HEAD_DIM
    score_buf = lambda dtype: pltpu.VMEM((HEADS_PER_KV, tk, tq), dtype)
    acc_bufs = [pltpu.VMEM((ACC_ROWS, tq), jnp.float32) for _ in range(HEADS_PER_KV)]
    in_specs = [
        pl.BlockSpec((1, HEADS_PER_KV, HEAD_DIM, tq), lambda b, g, i: (b, g, 0, i)),
        pl.BlockSpec((1, 1, seq_len, HEAD_DIM), lambda b, g, i: (b, g, 0, 0)),
        pl.BlockSpec((1, 1, n_chunks, HEAD_DIM, tk), lambda b, g, i: (b, g, 0, 0, 0)),
        pl.BlockSpec((tq, group_width), lambda b, g, i: (b * q_tiles + i, g)),
    ]
    common = dict(
        grid=(batch, N_KV_HEADS, q_tiles),
        out_specs=pl.BlockSpec((tq, group_width), lambda b, g, i: (b * q_tiles + i, g)),
        out_shape=jax.ShapeDtypeStruct((batch * seq_len, Q_WIDTH), jnp.bfloat16),
        compiler_params=pltpu.CompilerParams(
            dimension_semantics=("parallel", "parallel", "parallel"), vmem_limit_bytes=V7X_VMEM_LIMIT_BYTES),
    )

    def bounded(bound, *operands):
        return pl.pallas_call(
            functools.partial(_attention_bounded_kernel, n_chunks=n_chunks, kv_tile=tk),
            in_specs=[pl.BlockSpec((1, 1), lambda b, g, i: (0, 0))] + in_specs,
            scratch_shapes=[score_buf(jnp.bfloat16) for _ in range(2)]
                           + [pltpu.VMEM((HEAD_DIM, tq), jnp.float32) for _ in range(HEADS_PER_KV)],
            name="attention_bounded", **common,
        )(bound.reshape(1, 1), *operands)

    def general(bound, *operands):
        return pl.pallas_call(
            functools.partial(_attention_kernel, n_chunks=n_chunks, kv_tile=tk),
            in_specs=in_specs,
            scratch_shapes=[score_buf(jnp.float32) for _ in range(PIPE_SLOTS)]
                           + [score_buf(jnp.bfloat16) for _ in range(PIPE_SLOTS)] + acc_bufs,
            name="attention", **common,
        )(*operands)

    return lax.cond(score_bound <= BOUNDED_SOFTMAX_LIMIT, bounded, general, score_bound, q_h, k_h, v_t, zb)


def _window_sums(ext, window):
    n = ext.shape[0]
    back = lambda a, s: pltpu.roll(a, s, 0)
    fwd = lambda a, s: pltpu.roll(a, n - s, 0)
    half = window // 2
    run, span = ext, 1
    while span < half:
        run = run + back(run, span)
        span *= 2
    ahead = fwd(run, half - 1) if half > 1 else run
    total = ahead + back(run, 1)
    return total[POOL_HALO:n - POOL_HALO]


def _post_kernel(ua_ref, prev_ref, next_ref, za_ref, b_ref, ma_ref, mb_ref, x_ref, p_ref,
                 pool_w_ref, pool_scale_ref, wa_ref, wb_ref, wo_ref, npost_ref, nple_ref, wg_ref, wp_ref,
                 y_ref, *, seq_len, rows):
    i = pl.program_id(0)
    tiles_per_seq = seq_len // rows
    t0 = (i % tiles_per_seq) * rows
    f32, bf16 = jnp.float32, jnp.bfloat16
    dot = functools.partial(jnp.dot, preferred_element_type=f32)
    sub = rows // POST_SUBTILES
    parts = [slice(h * sub, (h + 1) * sub) for h in range(POST_SUBTILES)]

    b = [dot(b_ref[rs, :], wb_ref[...]) for rs in parts]

    ua = ua_ref[...]
    prev = jnp.where(t0 == 0, 0.0, prev_ref[...])
    nxt = jnp.where(t0 + rows == seq_len, 0.0, next_ref[...])
    ext = jnp.concatenate([prev, ua, nxt], axis=0)
    t = t0 + lax.broadcasted_iota(jnp.int32, (rows, 1), 0)
    pooled = []
    for gi, w in enumerate(POOL_WINDOWS):
        cols = slice(gi * POOL_GROUP_WIDTH, (gi + 1) * POOL_GROUP_WIDTH)
        total = _window_sums(ext[:, cols], w)
        lo = jnp.maximum(t - w // 2, 0)
        hi = jnp.minimum(t + (w - w // 2), seq_len)
        cnt = (hi - lo).astype(f32)
        pooled.append((total / cnt - ua[:, cols]).astype(bf16))
    mixed = [[dot(pooled[gi][rs], pool_w_ref[gi]) for gi in range(POOL_GROUPS)] for rs in parts]

    pe = [dot(p_ref[rs, :].astype(bf16), wp_ref[...]) for rs in parts]

    a = []
    for rs, mixed_h in zip(parts, mixed):
        za = za_ref[rs, :].astype(f32)
        a_in = jnp.concatenate(mixed_h, axis=1) * pool_scale_ref[...] * (za * jax.nn.sigmoid(za))
        a.append(dot(a_in.astype(bf16), wa_ref[...]))

    o = []
    for rs, a_h, b_h in zip(parts, a, b):
        ma = ma_ref[rs, :].astype(f32)
        mb = mb_ref[rs, :].astype(f32)
        merged = jax.nn.sigmoid(ma) * a_h + jax.nn.sigmoid(mb) * b_h
        o.append(dot(merged.astype(bf16), wo_ref[...]))

    x1, gate = [], []
    for rs, o_h in zip(parts, o):
        o_n = o_h * lax.rsqrt(jnp.mean(o_h * o_h, axis=-1, keepdims=True) + EPS) * npost_ref[...]
        x1_h = x_ref[rs, :] + o_n
        xn = x1_h * lax.rsqrt(jnp.mean(x1_h * x1_h, axis=-1, keepdims=True) + EPS) * nple_ref[...]
        x1.append(x1_h)
        gate.append(dot(xn.astype(bf16), wg_ref[...]))

    for rs, x1_h, gate_h, pe_h in zip(parts, x1, gate, pe):
        y_ref[rs, :] = x1_h + jax.nn.sigmoid(gate_h) * pe_h


def _post(ua, za, b_in, ma, mb, x2d, p2d, seq_len, weights, tiles):
    n_rows = x2d.shape[0]
    tm = tiles["post_rows"]
    halo_blocks = tm // POOL_HALO
    last_halo = n_rows // POOL_HALO - 1
    row = lambda width: pl.BlockSpec((tm, width), lambda i: (i, 0))
    const2 = lambda shape: pl.BlockSpec(shape, lambda i: (0, 0), pipeline_mode=pl.Buffered(1))
    kernel = functools.partial(_post_kernel, seq_len=seq_len, rows=tm)
    return pl.pallas_call(
        kernel,
        grid=(n_rows // tm,),
        in_specs=[
            row(POOL_WIDTH),
            pl.BlockSpec((POOL_HALO, POOL_WIDTH), lambda i: (jnp.maximum(i * halo_blocks - 1, 0), 0)),
            pl.BlockSpec((POOL_HALO, POOL_WIDTH), lambda i: (jnp.minimum((i + 1) * halo_blocks, last_halo), 0)),
            row(POOL_WIDTH), row(Q_WIDTH), row(D_MODEL), row(D_MODEL), row(D_MODEL), row(PLE_DIM),
            pl.BlockSpec((POOL_GROUPS, POOL_GROUP_WIDTH, POOL_GROUP_WIDTH), lambda i: (0, 0, 0),
                         pipeline_mode=pl.Buffered(1)),
            const2((1, POOL_WIDTH)),
            const2((POOL_WIDTH, D_MODEL)), const2((Q_WIDTH, D_MODEL)), const2((D_MODEL, D_MODEL)),
            const2((1, D_MODEL)), const2((1, D_MODEL)),
            const2((D_MODEL, D_MODEL)), const2((PLE_DIM, D_MODEL)),
        ],
        out_specs=row(D_MODEL),
        out_shape=jax.ShapeDtypeStruct((n_rows, D_MODEL), jnp.float32),
        compiler_params=pltpu.CompilerParams(
            dimension_semantics=("parallel",), vmem_limit_bytes=V7X_VMEM_LIMIT_BYTES),
        name="post",
    )(ua, ua, ua, za, b_in, ma, mb, x2d, p2d, *weights)


def _layer(x, p, norm_pre, w_in, pool_w, pool_scale, w_branch_a, q_gain, k_gain, w_branch_b, w_out,
           norm_post, ple_norm, w_ple_gate, w_ple_in):
    batch, seq_len, _ = x.shape
    tiles = _tiles(seq_len)
    x2d = x.reshape(batch * seq_len, D_MODEL)
    p2d = p.reshape(batch * seq_len, PLE_DIM)
    ua, za, q, k, v, zb, ma, mb = _in_proj(x2d, seq_len, norm_pre, w_in, q_gain, k_gain, tiles)
    score_bound = (HEAD_DIM * SCORE_SCALE_LOG2 * 1.02) * jnp.max(jnp.abs(q_gain)) * jnp.max(jnp.abs(k_gain))
    b_in = _attention(q, k, v, zb, score_bound, batch, seq_len, tiles)
    weights = (pool_w, pool_scale, w_branch_a, w_branch_b, w_out, norm_post, ple_norm, w_ple_gate, w_ple_in)
    y = _post(ua, za, b_in, ma, mb, x2d, p2d, seq_len, weights, tiles)
    return y.reshape(batch, seq_len, D_MODEL)


def kernel(x_prompt, x_sample, p_prompt, p_sample, norm_pre, w_in, pool_w, pool_scale, w_branch_a,
           q_norm, k_norm, w_branch_b, w_out, norm_post, ple_norm, w_ple_gate, w_ple_in):
    depth = w_in.shape[0]
    bf16 = jnp.bfloat16
    for layer in range(depth):
        params = (
            norm_pre[layer][None, :],
            w_in[layer].astype(bf16),
            pool_w[layer].astype(bf16),
            pool_scale[layer][None, :],
            w_branch_a[layer].astype(bf16),
            jnp.tile(q_norm[layer], V7X_LANES // HEAD_DIM)[None, :],
            jnp.tile(k_norm[layer], V7X_LANES // HEAD_DIM)[None, :],
            w_branch_b[layer].astype(bf16),
            w_out[layer].astype(bf16),
            norm_post[layer][None, :],
            ple_norm[layer][None, :],
            w_ple_gate[layer].astype(bf16),
            w_ple_in[layer].astype(bf16),
        )
        x_prompt = _layer(x_prompt, p_prompt[layer], *params)
        x_sample = _layer(x_sample, p_sample[layer], *params)
    return (x_prompt, x_sample)
```

```python
import functools
import math

import jax
import jax.numpy as jnp
from jax import lax
from jax.experimental import pallas as pl
from jax.experimental.pallas import tpu as pltpu

D_MODEL = 1024
GRID_W = 64
PLE_DIM = 256
POOL_WIDTH = 1024
POOL_GROUPS = 4
POOL_GROUP_WIDTH = POOL_WIDTH // POOL_GROUPS
POOL_WINDOWS = (2, 4, 8, 16)
N_Q_HEADS = 16
N_KV_HEADS = 4
HEADS_PER_KV = N_Q_HEADS // N_KV_HEADS
HEAD_DIM = 64
Q_WIDTH = N_Q_HEADS * HEAD_DIM
KV_WIDTH = N_KV_HEADS * HEAD_DIM
ROPE_AXIS_DIM = HEAD_DIM // 2
ROPE_HALF = ROPE_AXIS_DIM // 2
ROPE_BASE = 10000.0
EPS = 1e-6
IN_SPLITS = (POOL_WIDTH, POOL_WIDTH, Q_WIDTH, KV_WIDTH, KV_WIDTH, Q_WIDTH, D_MODEL, D_MODEL)
IN_WIDTH = sum(IN_SPLITS)
IN_OFFSETS = tuple(sum(IN_SPLITS[:i]) for i in range(len(IN_SPLITS)))

V7X_LANES = 128
V7X_SUBLANES = 8
V7X_VMEM_LIMIT_BYTES = 56 * 1024 * 1024

SCORE_SCALE_LOG2 = (HEAD_DIM ** -0.5) * math.log2(math.e)

POOL_HALO = 8
POST_SUBTILES = 2


def _tiles(seq_len):
    assert seq_len % 256 == 0
    return dict(proj_rows=256, post_rows=POST_SUBTILES * 256, q_tile=256, bounded_q_tile=256, kv_tile=256)


def _rope_tables(seq_len):
    t = jnp.arange(seq_len)
    pos_row = (t // GRID_W).astype(jnp.float32)
    pos_col = (t % GRID_W).astype(jnp.float32)
    freqs = ROPE_BASE ** (-jnp.arange(0, ROPE_AXIS_DIM, 2, dtype=jnp.float32) / ROPE_AXIS_DIM)
    ang_row = pos_row[:, None] * freqs[None, :]
    ang_col = pos_col[:, None] * freqs[None, :]
    zeros = jnp.zeros_like(ang_row)

    def head(fn_row, fn_col):
        return jnp.concatenate([fn_row, fn_row, fn_col, fn_col], axis=-1)

    cos = head(jnp.cos(ang_row), jnp.cos(ang_col))
    sin_lo = jnp.concatenate([zeros, jnp.sin(ang_row), zeros, jnp.sin(ang_col)], axis=-1)
    sin_hi = jnp.concatenate([-jnp.sin(ang_row), zeros, -jnp.sin(ang_col), zeros], axis=-1)
    two = lambda a: jnp.concatenate([a, a], axis=-1)
    return two(cos), two(sin_lo), two(sin_hi)


def _head_norm_rope(x, gain, cos, sin_lo, sin_hi):
    lane = lax.broadcasted_iota(jnp.int32, x.shape, 1)
    first = lane < HEAD_DIM
    x2 = x * x
    ss_a = jnp.sum(jnp.where(first, x2, 0.0), axis=-1, keepdims=True)
    ss_b = jnp.sum(jnp.where(first, 0.0, x2), axis=-1, keepdims=True)
    ms = jnp.where(first, ss_a, ss_b) * (1.0 / HEAD_DIM)
    xn = x * lax.rsqrt(ms + EPS) * gain
    return (xn * cos
            + pltpu.roll(xn, ROPE_HALF, 1) * sin_lo
            + pltpu.roll(xn, V7X_LANES - ROPE_HALF, 1) * sin_hi)


def _in_proj_kernel(x_ref, g_ref, w_ref, qg_ref, kg_ref, cos_ref, sin_lo_ref, sin_hi_ref,
                    ua_ref, za_ref, q_ref, k_ref, v_ref, zb_ref, ma_ref, mb_ref):
    x = x_ref[...]
    h = x * lax.rsqrt(jnp.mean(x * x, axis=-1, keepdims=True) + EPS) * g_ref[...]
    h = h.astype(jnp.bfloat16)

    def proj(idx):
        lo = IN_OFFSETS[idx]
        return jnp.dot(h, w_ref[:, lo:lo + IN_SPLITS[idx]], preferred_element_type=jnp.float32)

    ua_ref[...] = proj(0)
    za_ref[...] = proj(1).astype(za_ref.dtype)

    q = proj(2)
    rope = (cos_ref[...], sin_lo_ref[...], sin_hi_ref[...])
    for c in range(Q_WIDTH // V7X_LANES):
        sl = slice(c * V7X_LANES, (c + 1) * V7X_LANES)
        q_ref[0, sl, :] = _head_norm_rope(q[:, sl], qg_ref[...], *rope).T.astype(q_ref.dtype)

    k = proj(3)
    heads_per_vreg = V7X_LANES // HEAD_DIM
    for c in range(KV_WIDTH // V7X_LANES):
        sl = slice(c * V7X_LANES, (c + 1) * V7X_LANES)
        k_c = _head_norm_rope(k[:, sl], kg_ref[...], *rope).astype(k_ref.dtype)
        for j in range(heads_per_vreg):
            k_ref[0, c * heads_per_vreg + j] = k_c[:, j * HEAD_DIM:(j + 1) * HEAD_DIM]

    v_t = proj(4).T
    for g in range(N_KV_HEADS):
        v_ref[0, g, 0] = v_t[g * HEAD_DIM:(g + 1) * HEAD_DIM].astype(v_ref.dtype)
    zb_ref[...] = proj(5).astype(zb_ref.dtype)
    ma_ref[...] = proj(6).astype(ma_ref.dtype)
    mb_ref[...] = proj(7).astype(mb_ref.dtype)


def _in_proj(x2d, seq_len, norm_pre, w_in_bf16, q_gain_scaled, k_gain, rope_tables, tiles):
    n_rows = x2d.shape[0]
    batch = n_rows // seq_len
    tm = tiles["proj_rows"]
    assert tm == tiles["kv_tile"]
    seq_tiles = seq_len // tm
    assert all(t.shape[0] >= seq_len for t in rope_tables)

    row = lambda width: pl.BlockSpec((tm, width), lambda i: (i, 0))
    const = lambda shape: pl.BlockSpec(shape, lambda i: (0, 0))
    table = pl.BlockSpec((tm, V7X_LANES), lambda i: (i % seq_tiles, 0))
    bf16 = jnp.bfloat16
    out_shapes = (
        jax.ShapeDtypeStruct((n_rows, POOL_WIDTH), jnp.float32),
        jax.ShapeDtypeStruct((n_rows, POOL_WIDTH), bf16),
        jax.ShapeDtypeStruct((batch, Q_WIDTH, seq_len), bf16),
        jax.ShapeDtypeStruct((batch, N_KV_HEADS, seq_len, HEAD_DIM), bf16),
        jax.ShapeDtypeStruct((batch, N_KV_HEADS, seq_tiles, HEAD_DIM, tm), bf16),
        jax.ShapeDtypeStruct((n_rows, Q_WIDTH), bf16),
        jax.ShapeDtypeStruct((n_rows, D_MODEL), bf16),
        jax.ShapeDtypeStruct((n_rows, D_MODEL), bf16),
    )
    out_specs = (
        row(POOL_WIDTH), row(POOL_WIDTH),
        pl.BlockSpec((1, Q_WIDTH, tm), lambda i: (i // seq_tiles, 0, i % seq_tiles)),
        pl.BlockSpec((1, N_KV_HEADS, tm, HEAD_DIM), lambda i: (i // seq_tiles, 0, i % seq_tiles, 0)),
        pl.BlockSpec((1, N_KV_HEADS, 1, HEAD_DIM, tm), lambda i: (i // seq_tiles, 0, i % seq_tiles, 0, 0)),
        row(Q_WIDTH), row(D_MODEL), row(D_MODEL),
    )
    return pl.pallas_call(
        _in_proj_kernel,
        grid=(n_rows // tm,),
        in_specs=[row(D_MODEL), const((1, D_MODEL)), const((D_MODEL, IN_WIDTH)),
                  const((1, V7X_LANES)), const((1, V7X_LANES)),
                  table, table, table],
        out_specs=out_specs,
        out_shape=out_shapes,
        compiler_params=pltpu.CompilerParams(
            dimension_semantics=("parallel",), vmem_limit_bytes=V7X_VMEM_LIMIT_BYTES),
        name="in_proj",
    )(x2d, norm_pre, w_in_bf16, q_gain_scaled, k_gain, *rope_tables)


ACC_ROWS = HEAD_DIM + 16
CHUNKS_PER_ITER = 8
PIPE_SLOTS = 2


def _attention_kernel(q_ref, k_ref, vt_ref, zb_ref, o_ref, *scratch, n_chunks, kv_tile):
    s_refs, p_refs = scratch[:PIPE_SLOTS], scratch[PIPE_SLOTS:2 * PIPE_SLOTS]
    acc_refs = scratch[2 * PIPE_SLOTS:]
    tq = q_ref.shape[3]
    heads = range(HEADS_PER_KV)
    ones_rows = (lax.broadcasted_iota(jnp.int32, (ACC_ROWS - HEAD_DIM, kv_tile), 0) == 0).astype(jnp.bfloat16)

    def stage1(c, s_ref):
        start = pl.multiple_of(c * kv_tile, kv_tile)
        k_c = k_ref[0, 0, pl.ds(start, kv_tile), :]
        tile_max = []
        for hq in heads:
            s_t = jnp.dot(k_c, q_ref[0, hq], preferred_element_type=jnp.float32)
            s_ref[hq] = s_t
            m8 = s_t[:V7X_SUBLANES]
            for r in range(1, kv_tile // V7X_SUBLANES):
                m8 = jnp.maximum(m8, s_t[r * V7X_SUBLANES:(r + 1) * V7X_SUBLANES])
            tile_max.append(jnp.max(m8, axis=0, keepdims=True))
        return tuple(tile_max)

    def stage2(s_ref, tile_max, m_run, p_ref):
        m_out, alpha = [], []
        for hq in heads:
            m_new = jnp.maximum(m_run[hq], tile_max[hq])
            alpha.append(jnp.exp2(m_run[hq] - m_new))
            p_ref[hq] = jnp.exp2(s_ref[hq] - m_new).astype(jnp.bfloat16)
            m_out.append(m_new)
        return tuple(m_out), tuple(alpha)

    def stage3(c, p_ref, alpha):
        lhs = jnp.concatenate([vt_ref[0, 0, c], ones_rows], axis=0)
        for hq in heads:
            pv = jnp.dot(lhs, p_ref[hq], preferred_element_type=jnp.float32)
            acc_refs[hq][...] = alpha[hq] * acc_refs[hq][...] + pv

    for acc_ref in acc_refs:
        acc_ref[...] = jnp.zeros(acc_ref.shape, jnp.float32)
    prev_slot = PIPE_SLOTS - 1
    p_refs[prev_slot][...] = jnp.zeros(p_refs[prev_slot].shape, jnp.bfloat16)
    last = n_chunks - 1

    def body(j, carry):
        m_run, tile_max, alpha_prev = carry
        base = CHUNKS_PER_ITER * j
        for u in range(CHUNKS_PER_ITER):
            prv, cur, nxt = (u - 1) % PIPE_SLOTS, u % PIPE_SLOTS, (u + 1) % PIPE_SLOTS
            c = base + u
            stage3(jnp.maximum(c - 1, 0), p_refs[prv], alpha_prev)
            m_run, alpha_prev = stage2(s_refs[cur], tile_max, m_run, p_refs[cur])
            tile_max = stage1(jnp.minimum(c + 1, last), s_refs[nxt])
        return m_run, tile_max, alpha_prev

    init = (tuple(jnp.full((1, tq), -jnp.inf, jnp.float32) for _ in heads),
            stage1(0, s_refs[0]),
            tuple(jnp.ones((1, tq), jnp.float32) for _ in heads))
    _, _, alpha_last = lax.fori_loop(0, n_chunks // CHUNKS_PER_ITER, body, init)
    stage3(last, p_refs[prev_slot], alpha_last)

    parts = []
    for hq in heads:
        acc = acc_refs[hq][...]
        parts.append(acc[:HEAD_DIM] / acc[HEAD_DIM:HEAD_DIM + 1])
    o_t = jnp.concatenate(parts, axis=0)
    zb = zb_ref[...].astype(jnp.float32)
    o_ref[...] = (o_t.T * (zb * jax.nn.sigmoid(zb))).astype(o_ref.dtype)


def _attention_bounded_kernel(bound_ref, q_ref, k_ref, vt_ref, zb_ref, o_ref, *scratch, n_chunks, kv_tile):
    p_refs, acc_refs = scratch[:2], scratch[2:]
    n_sub = q_ref.shape[3] // BOUNDED_UNIT
    units = [(hq, j) for j in range(n_sub) for hq in range(HEADS_PER_KV)]
    bound = bound_ref[...]
    row_sums = [jnp.zeros((V7X_SUBLANES, BOUNDED_UNIT), jnp.float32) for _ in units]

    def stage_a(c, u, p_ref):
        hq, j = units[u]
        k_c = k_ref[0, 0, c * kv_tile:(c + 1) * kv_tile, :]
        q_u = q_ref[0, hq, :, j * BOUNDED_UNIT:(j + 1) * BOUNDED_UNIT]
        p_t = jnp.exp2(jnp.dot(k_c, q_u, preferred_element_type=jnp.float32) - bound)
        row_sums[u] = row_sums[u] + p_t.reshape(kv_tile // V7X_SUBLANES, V7X_SUBLANES, BOUNDED_UNIT).sum(axis=0)
        p_ref[u] = p_t.astype(jnp.bfloat16)

    def stage_b(c, u, p_ref):
        pv = jnp.dot(vt_ref[0, 0, c], p_ref[u], preferred_element_type=jnp.float32)
        acc_refs[u][...] = pv if c == 0 else acc_refs[u][...] + pv

    for u in range(len(units)):
        stage_a(0, u, p_refs[0])
    for c in range(n_chunks):
        for u in range(len(units)):
            if c + 1 < n_chunks:
                stage_a(c + 1, u, p_refs[(c + 1) % 2])
            stage_b(c, u, p_refs[c % 2])

    for j in range(n_sub):
        sub_units = range(j * HEADS_PER_KV, (j + 1) * HEADS_PER_KV)
        parts = [acc_refs[u][...] / jnp.sum(row_sums[u], axis=0, keepdims=True) for u in sub_units]
        o_t = jnp.concatenate(parts, axis=0)
        rows = slice(j * BOUNDED_UNIT, (j + 1) * BOUNDED_UNIT)
        zb = zb_ref[rows, :].astype(jnp.float32)
        o_ref[rows, :] = (o_t.T * (zb * jax.nn.sigmoid(zb))).astype(o_ref.dtype)


BOUNDED_SOFTMAX_LIMIT = 40.0
BOUNDED_UNIT = 256


def _attention(q_t, k_h, v_t, zb, score_bound, batch, seq_len, tiles):
    tk = tiles["kv_tile"]
    n_chunks = seq_len // tk
    assert n_chunks % CHUNKS_PER_ITER == 0 and CHUNKS_PER_ITER % PIPE_SLOTS == 0
    q_h = q_t.reshape(batch, N_Q_HEADS, HEAD_DIM, seq_len)
    group_width = HEADS_PER_KV * HEAD_DIM

    def specs(tq):
        q_tiles = seq_len // tq
        in_specs = [
            pl.BlockSpec((1, HEADS_PER_KV, HEAD_DIM, tq), lambda b, g, i: (b, g, 0, i)),
            pl.BlockSpec((1, 1, seq_len, HEAD_DIM), lambda b, g, i: (b, g, 0, 0)),
            pl.BlockSpec((1, 1, n_chunks, HEAD_DIM, tk), lambda b, g, i: (b, g, 0, 0, 0)),
            pl.BlockSpec((tq, group_width), lambda b, g, i: (b * q_tiles + i, g)),
        ]
        common = dict(
            grid=(batch, N_KV_HEADS, q_tiles),
            out_specs=pl.BlockSpec((tq, group_width), lambda b, g, i: (b * q_tiles + i, g)),
            out_shape=jax.ShapeDtypeStruct((batch * seq_len, Q_WIDTH), jnp.bfloat16),
            compiler_params=pltpu.CompilerParams(
                dimension_semantics=("parallel", "parallel", "parallel"), vmem_limit_bytes=V7X_VMEM_LIMIT_BYTES),
        )
        return in_specs, common

    def bounded(bound, *operands):
        tq = tiles["bounded_q_tile"]
        n_units = HEADS_PER_KV * (tq // BOUNDED_UNIT)
        in_specs, common = specs(tq)
        return pl.pallas_call(
            functools.partial(_attention_bounded_kernel, n_chunks=n_chunks, kv_tile=tk),
            in_specs=[pl.BlockSpec((1, 1), lambda b, g, i: (0, 0))] + in_specs,
            scratch_shapes=[pltpu.VMEM((n_units, tk, BOUNDED_UNIT), jnp.bfloat16) for _ in range(2)]
                           + [pltpu.VMEM((HEAD_DIM, BOUNDED_UNIT), jnp.float32) for _ in range(n_units)],
            name="attention_bounded", **common,
        )(bound.reshape(1, 1), *operands)

    def general(bound, *operands):
        tq = tiles["q_tile"]
        in_specs, common = specs(tq)
        score_buf = lambda dtype: pltpu.VMEM((HEADS_PER_KV, tk, tq), dtype)
        return pl.pallas_call(
            functools.partial(_attention_kernel, n_chunks=n_chunks, kv_tile=tk),
            in_specs=in_specs,
            scratch_shapes=[score_buf(jnp.float32) for _ in range(PIPE_SLOTS)]
                           + [score_buf(jnp.bfloat16) for _ in range(PIPE_SLOTS)]
                           + [pltpu.VMEM((ACC_ROWS, tq), jnp.float32) for _ in range(HEADS_PER_KV)],
            name="attention", **common,
        )(*operands)

    return lax.cond(score_bound <= BOUNDED_SOFTMAX_LIMIT, bounded, general, score_bound, q_h, k_h, v_t, zb)


def _window_sums(ext, window):
    n = ext.shape[0]
    back = lambda a, s: pltpu.roll(a, s, 0)
    fwd = lambda a, s: pltpu.roll(a, n - s, 0)
    half = window // 2
    run, span = ext, 1
    while span < half:
        run = run + back(run, span)
        span *= 2
    ahead = fwd(run, half - 1) if half > 1 else run
    total = ahead + back(run, 1)
    return total[POOL_HALO:n - POOL_HALO]


def _post_kernel(ua_ref, prev_ref, next_ref, za_ref, b_ref, ma_ref, mb_ref, x_ref, p_ref,
                 pool_w_ref, pool_scale_ref, wa_ref, wb_ref, wo_ref, npost_ref, nple_ref, wg_ref, wp_ref,
                 y_ref, *, seq_len, rows):
    i = pl.program_id(0)
    tiles_per_seq = seq_len // rows
    t0 = (i % tiles_per_seq) * rows
    f32, bf16 = jnp.float32, jnp.bfloat16
    dot = functools.partial(jnp.dot, preferred_element_type=f32)
    sub = rows // POST_SUBTILES
    parts = [slice(h * sub, (h + 1) * sub) for h in range(POST_SUBTILES)]

    b = [dot(b_ref[rs, :], wb_ref[...]) for rs in parts]

    ua = ua_ref[...]
    prev = jnp.where(t0 == 0, 0.0, prev_ref[...])
    nxt = jnp.where(t0 + rows == seq_len, 0.0, next_ref[...])
    ext = jnp.concatenate([prev, ua, nxt], axis=0)
    t = t0 + lax.broadcasted_iota(jnp.int32, (rows, 1), 0)
    pooled = []
    for gi, w in enumerate(POOL_WINDOWS):
        cols = slice(gi * POOL_GROUP_WIDTH, (gi + 1) * POOL_GROUP_WIDTH)
        total = _window_sums(ext[:, cols], w)
        lo = jnp.maximum(t - w // 2, 0)
        hi = jnp.minimum(t + (w - w // 2), seq_len)
        cnt = (hi - lo).astype(f32)
        pooled.append((total / cnt - ua[:, cols]).astype(bf16))
    mixed = [[dot(pooled[gi][rs], pool_w_ref[gi]) for gi in range(POOL_GROUPS)] for rs in parts]

    pe = [dot(p_ref[rs, :].astype(bf16), wp_ref[...]) for rs in parts]

    a = []
    for rs, mixed_h in zip(parts, mixed):
        za = za_ref[rs, :].astype(f32)
        a_in = jnp.concatenate(mixed_h, axis=1) * pool_scale_ref[...] * (za * jax.nn.sigmoid(za))
        a.append(dot(a_in.astype(bf16), wa_ref[...]))

    o = []
    for rs, a_h, b_h in zip(parts, a, b):
        ma = ma_ref[rs, :].astype(f32)
        mb = mb_ref[rs, :].astype(f32)
        merged = jax.nn.sigmoid(ma) * a_h + jax.nn.sigmoid(mb) * b_h
        o.append(dot(merged.astype(bf16), wo_ref[...]))

    x1, gate = [], []
    for rs, o_h in zip(parts, o):
        o_n = o_h * lax.rsqrt(jnp.mean(o_h * o_h, axis=-1, keepdims=True) + EPS) * npost_ref[...]
        x1_h = x_ref[rs, :] + o_n
        xn = x1_h * lax.rsqrt(jnp.mean(x1_h * x1_h, axis=-1, keepdims=True) + EPS) * nple_ref[...]
        x1.append(x1_h)
        gate.append(dot(xn.astype(bf16), wg_ref[...]))

    for rs, x1_h, gate_h, pe_h in zip(parts, x1, gate, pe):
        y_ref[rs, :] = x1_h + jax.nn.sigmoid(gate_h) * pe_h


def _post(ua, za, b_in, ma, mb, x2d, p2d, seq_len, weights, tiles):
    n_rows = x2d.shape[0]
    tm = tiles["post_rows"]
    halo_blocks = tm // POOL_HALO
    last_halo = n_rows // POOL_HALO - 1
    row = lambda width: pl.BlockSpec((tm, width), lambda i: (i, 0))
    const2 = lambda shape: pl.BlockSpec(shape, lambda i: (0, 0), pipeline_mode=pl.Buffered(1))
    kernel = functools.partial(_post_kernel, seq_len=seq_len, rows=tm)
    return pl.pallas_call(
        kernel,
        grid=(n_rows // tm,),
        in_specs=[
            row(POOL_WIDTH),
            pl.BlockSpec((POOL_HALO, POOL_WIDTH), lambda i: (jnp.maximum(i * halo_blocks - 1, 0), 0)),
            pl.BlockSpec((POOL_HALO, POOL_WIDTH), lambda i: (jnp.minimum((i + 1) * halo_blocks, last_halo), 0)),
            row(POOL_WIDTH), row(Q_WIDTH), row(D_MODEL), row(D_MODEL), row(D_MODEL), row(PLE_DIM),
            pl.BlockSpec((POOL_GROUPS, POOL_GROUP_WIDTH, POOL_GROUP_WIDTH), lambda i: (0, 0, 0),
                         pipeline_mode=pl.Buffered(1)),
            const2((1, POOL_WIDTH)),
            const2((POOL_WIDTH, D_MODEL)), const2((Q_WIDTH, D_MODEL)), const2((D_MODEL, D_MODEL)),
            const2((1, D_MODEL)), const2((1, D_MODEL)),
            const2((D_MODEL, D_MODEL)), const2((PLE_DIM, D_MODEL)),
        ],
        out_specs=row(D_MODEL),
        out_shape=jax.ShapeDtypeStruct((n_rows, D_MODEL), jnp.float32),
        compiler_params=pltpu.CompilerParams(
            dimension_semantics=("parallel",), vmem_limit_bytes=V7X_VMEM_LIMIT_BYTES),
        name="post",
    )(ua, ua, ua, za, b_in, ma, mb, x2d, p2d, *weights)


def _layer(x, p, rope_tables, norm_pre, w_in, pool_w, pool_scale, w_branch_a, q_gain, k_gain, w_branch_b, w_out,
           norm_post, ple_norm, w_ple_gate, w_ple_in):
    batch, seq_len, _ = x.shape
    tiles = _tiles(seq_len)
    x2d = x.reshape(batch * seq_len, D_MODEL)
    p2d = p.reshape(batch * seq_len, PLE_DIM)
    ua, za, q, k, v, zb, ma, mb = _in_proj(x2d, seq_len, norm_pre, w_in, q_gain * SCORE_SCALE_LOG2, k_gain,
                                           rope_tables, tiles)
    score_bound = (HEAD_DIM * SCORE_SCALE_LOG2 * 1.02) * jnp.max(jnp.abs(q_gain)) * jnp.max(jnp.abs(k_gain))
    b_in = _attention(q, k, v, zb, score_bound, batch, seq_len, tiles)
    weights = (pool_w, pool_scale, w_branch_a, w_branch_b, w_out, norm_post, ple_norm, w_ple_gate, w_ple_in)
    y = _post(ua, za, b_in, ma, mb, x2d, p2d, seq_len, weights, tiles)
    return y.reshape(batch, seq_len, D_MODEL)


def kernel(x_prompt, x_sample, p_prompt, p_sample, norm_pre, w_in, pool_w, pool_scale, w_branch_a,
           q_norm, k_norm, w_branch_b, w_out, norm_post, ple_norm, w_ple_gate, w_ple_in):
    depth = w_in.shape[0]
    bf16 = jnp.bfloat16
    rope_tables = _rope_tables(max(x_prompt.shape[1], x_sample.shape[1]))
    for layer in range(depth):
        params = (
            norm_pre[layer][None, :],
            w_in[layer].astype(bf16),
            pool_w[layer].astype(bf16),
            pool_scale[layer][None, :],
            w_branch_a[layer].astype(bf16),
            jnp.tile(q_norm[layer], V7X_LANES // HEAD_DIM)[None, :],
            jnp.tile(k_norm[layer], V7X_LANES // HEAD_DIM)[None, :],
            w_branch_b[layer].astype(bf16),
            w_out[layer].astype(bf16),
            norm_post[layer][None, :],
            ple_norm[layer][None, :],
            w_ple_gate[layer].astype(bf16),
            w_ple_in[layer].astype(bf16),
        )
        x_prompt = _layer(x_prompt, p_prompt[layer], rope_tables, *params)
        x_sample = _layer(x_sample, p_sample[layer], rope_tables, *params)
    return (x_prompt, x_sample)
```

```python
import functools
import math

import jax
import jax.numpy as jnp
from jax import lax
from jax.experimental import pallas as pl
from jax.experimental.pallas import tpu as pltpu

D_MODEL = 1024
GRID_W = 64
PLE_DIM = 256
POOL_WIDTH = 1024
POOL_GROUPS = 4
POOL_GROUP_WIDTH = POOL_WIDTH // POOL_GROUPS
POOL_WINDOWS = (2, 4, 8, 16)
N_Q_HEADS = 16
N_KV_HEADS = 4
HEADS_PER_KV = N_Q_HEADS // N_KV_HEADS
HEAD_DIM = 64
Q_WIDTH = N_Q_HEADS * HEAD_DIM
KV_WIDTH = N_KV_HEADS * HEAD_DIM
ROPE_AXIS_DIM = HEAD_DIM // 2
ROPE_HALF = ROPE_AXIS_DIM // 2
ROPE_BASE = 10000.0
EPS = 1e-6
IN_SPLITS = (POOL_WIDTH, POOL_WIDTH, Q_WIDTH, KV_WIDTH, KV_WIDTH, Q_WIDTH, D_MODEL, D_MODEL)
IN_WIDTH = sum(IN_SPLITS)
IN_OFFSETS = tuple(sum(IN_SPLITS[:i]) for i in range(len(IN_SPLITS)))

V7X_LANES = 128
V7X_SUBLANES = 8
V7X_VMEM_LIMIT_BYTES = 56 * 1024 * 1024

SCORE_SCALE_LOG2 = (HEAD_DIM ** -0.5) * math.log2(math.e)

POOL_HALO = 8
POST_SUBTILES = 2


def _tiles(seq_len):
    assert seq_len % 256 == 0
    bounded_q_tile = 512 if seq_len <= 2048 else 256
    return dict(proj_rows=256, post_rows=POST_SUBTILES * 256, q_tile=256, bounded_q_tile=bounded_q_tile, kv_tile=256)


def _rope_tables(seq_len):
    t = jnp.arange(seq_len)
    pos_row = (t // GRID_W).astype(jnp.float32)
    pos_col = (t % GRID_W).astype(jnp.float32)
    freqs = ROPE_BASE ** (-jnp.arange(0, ROPE_AXIS_DIM, 2, dtype=jnp.float32) / ROPE_AXIS_DIM)
    ang_row = pos_row[:, None] * freqs[None, :]
    ang_col = pos_col[:, None] * freqs[None, :]
    zeros = jnp.zeros_like(ang_row)

    def head(fn_row, fn_col):
        return jnp.concatenate([fn_row, fn_row, fn_col, fn_col], axis=-1)

    cos = head(jnp.cos(ang_row), jnp.cos(ang_col))
    sin_lo = jnp.concatenate([zeros, jnp.sin(ang_row), zeros, jnp.sin(ang_col)], axis=-1)
    sin_hi = jnp.concatenate([-jnp.sin(ang_row), zeros, -jnp.sin(ang_col), zeros], axis=-1)
    two = lambda a: jnp.concatenate([a, a], axis=-1)
    return two(cos), two(sin_lo), two(sin_hi)


def _head_norm_rope(x, gain, cos, sin_lo, sin_hi):
    lane = lax.broadcasted_iota(jnp.int32, x.shape, 1)
    first = lane < HEAD_DIM
    x2 = x * x
    ss_a = jnp.sum(jnp.where(first, x2, 0.0), axis=-1, keepdims=True)
    ss_b = jnp.sum(jnp.where(first, 0.0, x2), axis=-1, keepdims=True)
    ms = jnp.where(first, ss_a, ss_b) * (1.0 / HEAD_DIM)
    xn = x * lax.rsqrt(ms + EPS) * gain
    return (xn * cos
            + pltpu.roll(xn, ROPE_HALF, 1) * sin_lo
            + pltpu.roll(xn, V7X_LANES - ROPE_HALF, 1) * sin_hi)


def _in_proj_kernel(x_ref, g_ref, w_ref, qg_ref, kg_ref, cos_ref, sin_lo_ref, sin_hi_ref,
                    ua_ref, za_ref, q_ref, k_ref, v_ref, zb_ref, ma_ref, mb_ref):
    x = x_ref[...]
    h = x * lax.rsqrt(jnp.mean(x * x, axis=-1, keepdims=True) + EPS) * g_ref[...]
    h = h.astype(jnp.bfloat16)

    def proj(idx):
        lo = IN_OFFSETS[idx]
        return jnp.dot(h, w_ref[:, lo:lo + IN_SPLITS[idx]], preferred_element_type=jnp.float32)

    ua_ref[...] = proj(0)
    za_ref[...] = proj(1).astype(za_ref.dtype)

    q = proj(2)
    rope = (cos_ref[...], sin_lo_ref[...], sin_hi_ref[...])
    for c in range(Q_WIDTH // V7X_LANES):
        sl = slice(c * V7X_LANES, (c + 1) * V7X_LANES)
        q_ref[0, sl, :] = _head_norm_rope(q[:, sl], qg_ref[...], *rope).T.astype(q_ref.dtype)

    k = proj(3)
    heads_per_vreg = V7X_LANES // HEAD_DIM
    for c in range(KV_WIDTH // V7X_LANES):
        sl = slice(c * V7X_LANES, (c + 1) * V7X_LANES)
        k_c = _head_norm_rope(k[:, sl], kg_ref[...], *rope).astype(k_ref.dtype)
        for j in range(heads_per_vreg):
            k_ref[0, c * heads_per_vreg + j] = k_c[:, j * HEAD_DIM:(j + 1) * HEAD_DIM]

    v_t = proj(4).T
    for g in range(N_KV_HEADS):
        v_ref[0, g, 0] = v_t[g * HEAD_DIM:(g + 1) * HEAD_DIM].astype(v_ref.dtype)
    zb_ref[...] = proj(5).astype(zb_ref.dtype)
    ma_ref[...] = proj(6).astype(ma_ref.dtype)
    mb_ref[...] = proj(7).astype(mb_ref.dtype)


def _in_proj(x2d, seq_len, norm_pre, w_in_bf16, q_gain_scaled, k_gain, rope_tables, tiles):
    n_rows = x2d.shape[0]
    batch = n_rows // seq_len
    tm = tiles["proj_rows"]
    assert tm == tiles["kv_tile"]
    seq_tiles = seq_len // tm
    assert all(t.shape[0] >= seq_len for t in rope_tables)

    row = lambda width: pl.BlockSpec((tm, width), lambda i: (i, 0))
    const = lambda shape: pl.BlockSpec(shape, lambda i: (0, 0))
    table = pl.BlockSpec((tm, V7X_LANES), lambda i: (i % seq_tiles, 0))
    bf16 = jnp.bfloat16
    out_shapes = (
        jax.ShapeDtypeStruct((n_rows, POOL_WIDTH), jnp.float32),
        jax.ShapeDtypeStruct((n_rows, POOL_WIDTH), bf16),
        jax.ShapeDtypeStruct((batch, Q_WIDTH, seq_len), bf16),
        jax.ShapeDtypeStruct((batch, N_KV_HEADS, seq_len, HEAD_DIM), bf16),
        jax.ShapeDtypeStruct((batch, N_KV_HEADS, seq_tiles, HEAD_DIM, tm), bf16),
        jax.ShapeDtypeStruct((n_rows, Q_WIDTH), bf16),
        jax.ShapeDtypeStruct((n_rows, D_MODEL), bf16),
        jax.ShapeDtypeStruct((n_rows, D_MODEL), bf16),
    )
    out_specs = (
        row(POOL_WIDTH), row(POOL_WIDTH),
        pl.BlockSpec((1, Q_WIDTH, tm), lambda i: (i // seq_tiles, 0, i % seq_tiles)),
        pl.BlockSpec((1, N_KV_HEADS, tm, HEAD_DIM), lambda i: (i // seq_tiles, 0, i % seq_tiles, 0)),
        pl.BlockSpec((1, N_KV_HEADS, 1, HEAD_DIM, tm), lambda i: (i // seq_tiles, 0, i % seq_tiles, 0, 0)),
        row(Q_WIDTH), row(D_MODEL), row(D_MODEL),
    )
    return pl.pallas_call(
        _in_proj_kernel,
        grid=(n_rows // tm,),
        in_specs=[row(D_MODEL), const((1, D_MODEL)), const((D_MODEL, IN_WIDTH)),
                  const((1, V7X_LANES)), const((1, V7X_LANES)),
                  table, table, table],
        out_specs=out_specs,
        out_shape=out_shapes,
        compiler_params=pltpu.CompilerParams(
            dimension_semantics=("parallel",), vmem_limit_bytes=V7X_VMEM_LIMIT_BYTES),
        name="in_proj",
    )(x2d, norm_pre, w_in_bf16, q_gain_scaled, k_gain, *rope_tables)


ACC_ROWS = HEAD_DIM + 16
CHUNKS_PER_ITER = 8
PIPE_SLOTS = 2


def _attention_kernel(q_ref, k_ref, vt_ref, zb_ref, o_ref, *scratch, n_chunks, kv_tile):
    s_refs, p_refs = scratch[:PIPE_SLOTS], scratch[PIPE_SLOTS:2 * PIPE_SLOTS]
    acc_refs = scratch[2 * PIPE_SLOTS:]
    tq = q_ref.shape[3]
    heads = range(HEADS_PER_KV)
    ones_rows = (lax.broadcasted_iota(jnp.int32, (ACC_ROWS - HEAD_DIM, kv_tile), 0) == 0).astype(jnp.bfloat16)

    def stage1(c, s_ref):
        start = pl.multiple_of(c * kv_tile, kv_tile)
        k_c = k_ref[0, 0, pl.ds(start, kv_tile), :]
        tile_max = []
        for hq in heads:
            s_t = jnp.dot(k_c, q_ref[0, hq], preferred_element_type=jnp.float32)
            s_ref[hq] = s_t
            m8 = s_t[:V7X_SUBLANES]
            for r in range(1, kv_tile // V7X_SUBLANES):
                m8 = jnp.maximum(m8, s_t[r * V7X_SUBLANES:(r + 1) * V7X_SUBLANES])
            tile_max.append(jnp.max(m8, axis=0, keepdims=True))
        return tuple(tile_max)

    def stage2(s_ref, tile_max, m_run, p_ref):
        m_out, alpha = [], []
        for hq in heads:
            m_new = jnp.maximum(m_run[hq], tile_max[hq])
            alpha.append(jnp.exp2(m_run[hq] - m_new))
            p_ref[hq] = jnp.exp2(s_ref[hq] - m_new).astype(jnp.bfloat16)
            m_out.append(m_new)
        return tuple(m_out), tuple(alpha)

    def stage3(c, p_ref, alpha):
        lhs = jnp.concatenate([vt_ref[0, 0, c], ones_rows], axis=0)
        for hq in heads:
            pv = jnp.dot(lhs, p_ref[hq], preferred_element_type=jnp.float32)
            acc_refs[hq][...] = alpha[hq] * acc_refs[hq][...] + pv

    for acc_ref in acc_refs:
        acc_ref[...] = jnp.zeros(acc_ref.shape, jnp.float32)
    prev_slot = PIPE_SLOTS - 1
    p_refs[prev_slot][...] = jnp.zeros(p_refs[prev_slot].shape, jnp.bfloat16)
    last = n_chunks - 1

    def body(j, carry):
        m_run, tile_max, alpha_prev = carry
        base = CHUNKS_PER_ITER * j
        for u in range(CHUNKS_PER_ITER):
            prv, cur, nxt = (u - 1) % PIPE_SLOTS, u % PIPE_SLOTS, (u + 1) % PIPE_SLOTS
            c = base + u
            stage3(jnp.maximum(c - 1, 0), p_refs[prv], alpha_prev)
            m_run, alpha_prev = stage2(s_refs[cur], tile_max, m_run, p_refs[cur])
            tile_max = stage1(jnp.minimum(c + 1, last), s_refs[nxt])
        return m_run, tile_max, alpha_prev

    init = (tuple(jnp.full((1, tq), -jnp.inf, jnp.float32) for _ in heads),
            stage1(0, s_refs[0]),
            tuple(jnp.ones((1, tq), jnp.float32) for _ in heads))
    _, _, alpha_last = lax.fori_loop(0, n_chunks // CHUNKS_PER_ITER, body, init)
    stage3(last, p_refs[prev_slot], alpha_last)

    parts = []
    for hq in heads:
        acc = acc_refs[hq][...]
        parts.append(acc[:HEAD_DIM] / acc[HEAD_DIM:HEAD_DIM + 1])
    o_t = jnp.concatenate(parts, axis=0)
    zb = zb_ref[...].astype(jnp.float32)
    o_ref[...] = (o_t.T * (zb * jax.nn.sigmoid(zb))).astype(o_ref.dtype)


def _attention_bounded_kernel(bound_ref, q_ref, k_ref, vt_ref, zb_ref, o_ref, *scratch, n_chunks, kv_tile):
    p_refs, acc_refs = scratch[:2], scratch[2:]
    n_sub = q_ref.shape[3] // BOUNDED_UNIT
    units = [(hq, j) for j in range(n_sub) for hq in range(HEADS_PER_KV)]
    bound = bound_ref[...]
    row_sums = [jnp.zeros((V7X_SUBLANES, BOUNDED_UNIT), jnp.float32) for _ in units]

    def stage_a(c, u, p_ref):
        hq, j = units[u]
        k_c = k_ref[0, 0, c * kv_tile:(c + 1) * kv_tile, :]
        q_u = q_ref[0, hq, :, j * BOUNDED_UNIT:(j + 1) * BOUNDED_UNIT]
        p_t = jnp.exp2(jnp.dot(k_c, q_u, preferred_element_type=jnp.float32) - bound)
        row_sums[u] = row_sums[u] + p_t.reshape(kv_tile // V7X_SUBLANES, V7X_SUBLANES, BOUNDED_UNIT).sum(axis=0)
        p_ref[u] = p_t.astype(jnp.bfloat16)

    def stage_b(c, u, p_ref):
        pv = jnp.dot(vt_ref[0, 0, c], p_ref[u], preferred_element_type=jnp.float32)
        acc_refs[u][...] = pv if c == 0 else acc_refs[u][...] + pv

    for u in range(len(units)):
        stage_a(0, u, p_refs[0])
    for c in range(n_chunks):
        for u in range(len(units)):
            if c + 1 < n_chunks:
                stage_a(c + 1, u, p_refs[(c + 1) % 2])
            stage_b(c, u, p_refs[c % 2])

    for j in range(n_sub):
        sub_units = range(j * HEADS_PER_KV, (j + 1) * HEADS_PER_KV)
        parts = [acc_refs[u][...] / jnp.sum(row_sums[u], axis=0, keepdims=True) for u in sub_units]
        o_t = jnp.concatenate(parts, axis=0)
        rows = slice(j * BOUNDED_UNIT, (j + 1) * BOUNDED_UNIT)
        zb = zb_ref[rows, :].astype(jnp.float32)
        o_ref[rows, :] = (o_t.T * (zb * jax.nn.sigmoid(zb))).astype(o_ref.dtype)


BOUNDED_SOFTMAX_LIMIT = 40.0
BOUNDED_UNIT = 256


def _attention(q_t, k_h, v_t, zb, score_bound, batch, seq_len, tiles):
    tk = tiles["kv_tile"]
    n_chunks = seq_len // tk
    assert n_chunks % CHUNKS_PER_ITER == 0 and CHUNKS_PER_ITER % PIPE_SLOTS == 0
    q_h = q_t.reshape(batch, N_Q_HEADS, HEAD_DIM, seq_len)
    group_width = HEADS_PER_KV * HEAD_DIM

    def specs(tq):
        q_tiles = seq_len // tq
        in_specs = [
            pl.BlockSpec((1, HEADS_PER_KV, HEAD_DIM, tq), lambda b, g, i: (b, g, 0, i)),
            pl.BlockSpec((1, 1, seq_len, HEAD_DIM), lambda b, g, i: (b, g, 0, 0)),
            pl.BlockSpec((1, 1, n_chunks, HEAD_DIM, tk), lambda b, g, i: (b, g, 0, 0, 0)),
            pl.BlockSpec((tq, group_width), lambda b, g, i: (b * q_tiles + i, g)),
        ]
        common = dict(
            grid=(batch, N_KV_HEADS, q_tiles),
            out_specs=pl.BlockSpec((tq, group_width), lambda b, g, i: (b * q_tiles + i, g)),
            out_shape=jax.ShapeDtypeStruct((batch * seq_len, Q_WIDTH), jnp.bfloat16),
            compiler_params=pltpu.CompilerParams(
                dimension_semantics=("parallel", "parallel", "parallel"), vmem_limit_bytes=V7X_VMEM_LIMIT_BYTES),
        )
        return in_specs, common

    def bounded(bound, *operands):
        tq = tiles["bounded_q_tile"]
        n_units = HEADS_PER_KV * (tq // BOUNDED_UNIT)
        in_specs, common = specs(tq)
        return pl.pallas_call(
            functools.partial(_attention_bounded_kernel, n_chunks=n_chunks, kv_tile=tk),
            in_specs=[pl.BlockSpec((1, 1), lambda b, g, i: (0, 0))] + in_specs,
            scratch_shapes=[pltpu.VMEM((n_units, tk, BOUNDED_UNIT), jnp.bfloat16) for _ in range(2)]
                           + [pltpu.VMEM((HEAD_DIM, BOUNDED_UNIT), jnp.float32) for _ in range(n_units)],
            name="attention_bounded", **common,
        )(bound.reshape(1, 1), *operands)

    def general(bound, *operands):
        tq = tiles["q_tile"]
        in_specs, common = specs(tq)
        score_buf = lambda dtype: pltpu.VMEM((HEADS_PER_KV, tk, tq), dtype)
        return pl.pallas_call(
            functools.partial(_attention_kernel, n_chunks=n_chunks, kv_tile=tk),
            in_specs=in_specs,
            scratch_shapes=[score_buf(jnp.float32) for _ in range(PIPE_SLOTS)]
                           + [score_buf(jnp.bfloat16) for _ in range(PIPE_SLOTS)]
                           + [pltpu.VMEM((ACC_ROWS, tq), jnp.float32) for _ in range(HEADS_PER_KV)],
            name="attention", **common,
        )(*operands)

    return lax.cond(score_bound <= BOUNDED_SOFTMAX_LIMIT, bounded, general, score_bound, q_h, k_h, v_t, zb)


def _window_sums(ext, window):
    n = ext.shape[0]
    back = lambda a, s: pltpu.roll(a, s, 0)
    fwd = lambda a, s: pltpu.roll(a, n - s, 0)
    half = window // 2
    run, span = ext, 1
    while span < half:
        run = run + back(run, span)
        span *= 2
    ahead = fwd(run, half - 1) if half > 1 else run
    total = ahead + back(run, 1)
    return total[POOL_HALO:n - POOL_HALO]


def _post_kernel(ua_ref, prev_ref, next_ref, za_ref, b_ref, ma_ref, mb_ref, x_ref, p_ref,
                 pool_w_ref, pool_scale_ref, wa_ref, wb_ref, wo_ref, npost_ref, nple_ref, wg_ref, wp_ref,
                 y_ref, *, seq_len, rows):
    i = pl.program_id(0)
    tiles_per_seq = seq_len // rows
    t0 = (i % tiles_per_seq) * rows
    f32, bf16 = jnp.float32, jnp.bfloat16
    dot = functools.partial(jnp.dot, preferred_element_type=f32)
    sub = rows // POST_SUBTILES
    parts = [slice(h * sub, (h + 1) * sub) for h in range(POST_SUBTILES)]

    b = [dot(b_ref[rs, :], wb_ref[...]) for rs in parts]

    ua = ua_ref[...]
    prev = jnp.where(t0 == 0, 0.0, prev_ref[...])
    nxt = jnp.where(t0 + rows == seq_len, 0.0, next_ref[...])
    ext = jnp.concatenate([prev, ua, nxt], axis=0)
    t = t0 + lax.broadcasted_iota(jnp.int32, (rows, 1), 0)
    pooled = []
    for gi, w in enumerate(POOL_WINDOWS):
        cols = slice(gi * POOL_GROUP_WIDTH, (gi + 1) * POOL_GROUP_WIDTH)
        total = _window_sums(ext[:, cols], w)
        lo = jnp.maximum(t - w // 2, 0)
        hi = jnp.minimum(t + (w - w // 2), seq_len)
        cnt = (hi - lo).astype(f32)
        pooled.append((total / cnt - ua[:, cols]).astype(bf16))
    mixed = [[dot(pooled[gi][rs], pool_w_ref[gi]) for gi in range(POOL_GROUPS)] for rs in parts]

    pe = [dot(p_ref[rs, :].astype(bf16), wp_ref[...]) for rs in parts]

    a = []
    for rs, mixed_h in zip(parts, mixed):
        za = za_ref[rs, :].astype(f32)
        a_in = jnp.concatenate(mixed_h, axis=1) * pool_scale_ref[...] * (za * jax.nn.sigmoid(za))
        a.append(dot(a_in.astype(bf16), wa_ref[...]))

    o = []
    for rs, a_h, b_h in zip(parts, a, b):
        ma = ma_ref[rs, :].astype(f32)
        mb = mb_ref[rs, :].astype(f32)
        merged = jax.nn.sigmoid(ma) * a_h + jax.nn.sigmoid(mb) * b_h
        o.append(dot(merged.astype(bf16), wo_ref[...]))

    x1, gate = [], []
    for rs, o_h in zip(parts, o):
        o_n = o_h * lax.rsqrt(jnp.mean(o_h * o_h, axis=-1, keepdims=True) + EPS) * npost_ref[...]
        x1_h = x_ref[rs, :] + o_n
        xn = x1_h * lax.rsqrt(jnp.mean(x1_h * x1_h, axis=-1, keepdims=True) + EPS) * nple_ref[...]
        x1.append(x1_h)
        gate.append(dot(xn.astype(bf16), wg_ref[...]))

    for rs, x1_h, gate_h, pe_h in zip(parts, x1, gate, pe):
        y_ref[rs, :] = x1_h + jax.nn.sigmoid(gate_h) * pe_h


def _post(ua, za, b_in, ma, mb, x2d, p2d, seq_len, weights, tiles):
    n_rows = x2d.shape[0]
    tm = tiles["post_rows"]
    halo_blocks = tm // POOL_HALO
    last_halo = n_rows // POOL_HALO - 1
    row = lambda width: pl.BlockSpec((tm, width), lambda i: (i, 0))
    const2 = lambda shape: pl.BlockSpec(shape, lambda i: (0, 0), pipeline_mode=pl.Buffered(1))
    kernel = functools.partial(_post_kernel, seq_len=seq_len, rows=tm)
    return pl.pallas_call(
        kernel,
        grid=(n_rows // tm,),
        in_specs=[
            row(POOL_WIDTH),
            pl.BlockSpec((POOL_HALO, POOL_WIDTH), lambda i: (jnp.maximum(i * halo_blocks - 1, 0), 0)),
            pl.BlockSpec((POOL_HALO, POOL_WIDTH), lambda i: (jnp.minimum((i + 1) * halo_blocks, last_halo), 0)),
            row(POOL_WIDTH), row(Q_WIDTH), row(D_MODEL), row(D_MODEL), row(D_MODEL), row(PLE_DIM),
            pl.BlockSpec((POOL_GROUPS, POOL_GROUP_WIDTH, POOL_GROUP_WIDTH), lambda i: (0, 0, 0),
                         pipeline_mode=pl.Buffered(1)),
            const2((1, POOL_WIDTH)),
            const2((POOL_WIDTH, D_MODEL)), const2((Q_WIDTH, D_MODEL)), const2((D_MODEL, D_MODEL)),
            const2((1, D_MODEL)), const2((1, D_MODEL)),
            const2((D_MODEL, D_MODEL)), const2((PLE_DIM, D_MODEL)),
        ],
        out_specs=row(D_MODEL),
        out_shape=jax.ShapeDtypeStruct((n_rows, D_MODEL), jnp.float32),
        compiler_params=pltpu.CompilerParams(
            dimension_semantics=("parallel",), vmem_limit_bytes=V7X_VMEM_LIMIT_BYTES),
        name="post",
    )(ua, ua, ua, za, b_in, ma, mb, x2d, p2d, *weights)


def _layer(x, p, rope_tables, norm_pre, w_in, pool_w, pool_scale, w_branch_a, q_gain, k_gain, w_branch_b, w_out,
           norm_post, ple_norm, w_ple_gate, w_ple_in):
    batch, seq_len, _ = x.shape
    tiles = _tiles(seq_len)
    x2d = x.reshape(batch * seq_len, D_MODEL)
    p2d = p.reshape(batch * seq_len, PLE_DIM)
    ua, za, q, k, v, zb, ma, mb = _in_proj(x2d, seq_len, norm_pre, w_in, q_gain * SCORE_SCALE_LOG2, k_gain,
                                           rope_tables, tiles)
    score_bound = (HEAD_DIM * SCORE_SCALE_LOG2 * 1.02) * jnp.max(jnp.abs(q_gain)) * jnp.max(jnp.abs(k_gain))
    b_in = _attention(q, k, v, zb, score_bound, batch, seq_len, tiles)
    weights = (pool_w, pool_scale, w_branch_a, w_branch_b, w_out, norm_post, ple_norm, w_ple_gate, w_ple_in)
    y = _post(ua, za, b_in, ma, mb, x2d, p2d, seq_len, weights, tiles)
    return y.reshape(batch, seq_len, D_MODEL)


def kernel(x_prompt, x_sample, p_prompt, p_sample, norm_pre, w_in, pool_w, pool_scale, w_branch_a,
           q_norm, k_norm, w_branch_b, w_out, norm_post, ple_norm, w_ple_gate, w_ple_in):
    depth = w_in.shape[0]
    bf16 = jnp.bfloat16
    rope_tables = _rope_tables(max(x_prompt.shape[1], x_sample.shape[1]))
    for layer in range(depth):
        params = (
            norm_pre[layer][None, :],
            w_in[layer].astype(bf16),
            pool_w[layer].astype(bf16),
            pool_scale[layer][None, :],
            w_branch_a[layer].astype(bf16),
            jnp.tile(q_norm[layer], V7X_LANES // HEAD_DIM)[None, :],
            jnp.tile(k_norm[layer], V7X_LANES // HEAD_DIM)[None, :],
            w_branch_b[layer].astype(bf16),
            w_out[layer].astype(bf16),
            norm_post[layer][None, :],
            ple_norm[layer][None, :],
            w_ple_gate[layer].astype(bf16),
            w_ple_in[layer].astype(bf16),
        )
        x_prompt = _layer(x_prompt, p_prompt[layer], rope_tables, *params)
        x_sample = _layer(x_sample, p_sample[layer], rope_tables, *params)
    return (x_prompt, x_sample)
```
